```python
import jax, jax.numpy as jnp
from jax import lax
import numpy as np

D_MODEL = 2048
BATCH = 4
SEQ = 4096
DEPTH = 1

CHUNK = 64
Q_BLOCK = 128
RET_HEADS = D_MODEL // 256
RET_DK = 128
RET_DV = 128
RET_QK = RET_HEADS * RET_DK
RET_WIDTH = RET_HEADS * RET_DV
MLA_HEADS = D_MODEL // 256
MLA_NOPE = 128
MLA_ROPE = 64
MLA_DV = 128
MLA_Q_RANK = 768
MLA_KV_RANK = 512
MLA_WIDTH = MLA_HEADS * MLA_DV
MIX_WIDTH = RET_WIDTH + MLA_WIDTH
IN_SIZES = (RET_QK, RET_QK, RET_WIDTH, RET_WIDTH, MLA_Q_RANK, MLA_KV_RANK, MLA_ROPE)
IN_WIDTH = RET_QK * 2 + RET_WIDTH * 2 + MLA_Q_RANK + MLA_KV_RANK + MLA_ROPE
D_FF = 4 * D_MODEL
ROPE_BASE = 10000.0
EPS = 1e-5
DEEPNORM_ALPHA = (2.0 * DEPTH) ** 0.25
DEEPNORM_BETA = (8.0 * DEPTH) ** -0.25

kernel_name = 'hymba_retention_mla_deepnorm_block'


def _layer_norm(x, g, b):
    xf = x.astype(jnp.float32)
    mu = jnp.mean(xf, axis=-1, keepdims=True)
    var = jnp.mean(jnp.square(xf - mu), axis=-1, keepdims=True)
    return ((xf - mu) * lax.rsqrt(var + EPS) * g + b).astype(x.dtype)


def _rms_norm(x, g):
    xf = x.astype(jnp.float32)
    return (xf * lax.rsqrt(jnp.mean(jnp.square(xf), axis=-1, keepdims=True) + EPS) * g).astype(x.dtype)


def _rope(x, positions):
    d = x.shape[-1]
    inv = ROPE_BASE ** (-jnp.arange(0, d, 2, dtype=jnp.float32) / d)
    ang = positions.astype(jnp.float32)[..., None] * inv
    ang = ang.reshape(ang.shape[:2] + (1,) * (x.ndim - 3) + ang.shape[-1:])
    cos, sin = jnp.cos(ang), jnp.sin(ang)
    xf = x.astype(jnp.float32)
    x1, x2 = xf[..., : d // 2], xf[..., d // 2:]
    return jnp.concatenate([x1 * cos - x2 * sin, x1 * sin + x2 * cos], axis=-1).astype(x.dtype)


def _retention(q, k, v):
    B, S, H, dk = q.shape
    dv = v.shape[-1]
    nc = S // CHUNK
    dt = q.dtype
    q = q.reshape(B, nc, CHUNK, H, dk).transpose(0, 3, 1, 2, 4)
    k = k.reshape(B, nc, CHUNK, H, dk).transpose(0, 3, 1, 2, 4) * (dk ** -0.5)
    v = v.reshape(B, nc, CHUNK, H, dv).transpose(0, 3, 1, 2, 4)
    log_g = jnp.log1p(-jnp.exp2(-5.0 - jnp.arange(H, dtype=jnp.float32)))
    idx = jnp.arange(CHUNK, dtype=jnp.float32)
    intra_decay = jnp.exp(log_g[:, None, None] * jnp.abs(idx[:, None] - idx[None, :])).astype(dt)
    k_decay = jnp.exp(log_g[:, None] * (CHUNK - 1 - idx)).astype(dt)
    q_decay = jnp.exp(log_g[:, None] * (idx + 1.0)).astype(dt)
    chunk_decay = jnp.exp(log_g * CHUNK).astype(dt)[:, None, None]
    scores = jnp.einsum('bhnid,bhnjd->bhnij', q, k) * intra_decay[:, None]
    intra = jnp.einsum('bhnij,bhnje->bhnie', scores, v)
    kv = jnp.einsum('bhnjd,bhnje->nbhde', k * k_decay[:, None, :, None], v)

    def step(state, kv_c):
        return chunk_decay * state + kv_c, state

    _, s_prev = lax.scan(step, jnp.zeros((B, H, dk, dv), kv.dtype), kv)
    cross = jnp.einsum('bhnid,nbhde->bhnie', q * q_decay[:, None, :, None], s_prev)
    return (intra + cross).transpose(0, 2, 3, 1, 4).reshape(B, S, H, dv)


def _mla(c_q, c_kv, k_r, positions, q_norm_g, w_uq, kv_norm_g, w_uk, w_uv):
    B, S, _ = c_q.shape
    H = MLA_HEADS
    q = (_rms_norm(c_q, q_norm_g) @ w_uq).reshape(B, S, H, MLA_NOPE + MLA_ROPE)
    q_nope = q[..., :MLA_NOPE]
    q_rope = _rope(q[..., MLA_NOPE:], positions)
    ckv = _rms_norm(c_kv, kv_norm_g)
    k_nope = (ckv @ w_uk).reshape(B, S, H, MLA_NOPE)
    v = (ckv @ w_uv).reshape(B, S, H, MLA_DV)
    k_rope = _rope(k_r, positions)
    scale = (MLA_NOPE + MLA_ROPE) ** -0.5
    nqb = S // Q_BLOCK
    qn = q_nope.reshape(B, nqb, Q_BLOCK, H, MLA_NOPE).transpose(1, 0, 3, 2, 4)
    qr = q_rope.reshape(B, nqb, Q_BLOCK, H, MLA_ROPE).transpose(1, 0, 3, 2, 4)
    k_chunk = jnp.arange(S) // CHUNK

    def block(args):
        qn_b, qr_b, start = args
        s = (jnp.einsum('bhqd,bkhd->bhqk', qn_b, k_nope)
             + jnp.einsum('bhqd,bkd->bhqk', qr_b, k_rope)).astype(jnp.float32) * scale
        q_chunk = (start + jnp.arange(Q_BLOCK)) // CHUNK
        mask = k_chunk[None, :] <= q_chunk[:, None]
        p = jax.nn.softmax(jnp.where(mask, s, -jnp.inf), axis=-1).astype(v.dtype)
        return jnp.einsum('bhqk,bkhd->bqhd', p, v)

    out = lax.map(block, (qn, qr, jnp.arange(nqb, dtype=jnp.int32) * Q_BLOCK))
    return out.transpose(1, 0, 2, 3, 4).reshape(B, S, H * MLA_DV)


def setup_inputs(seed: int = 0) -> dict:
    key = jax.random.key(seed)
    ks = jax.random.split(key, 16)
    f32 = jnp.float32
    x = jax.random.normal(ks[0], (BATCH, SEQ, D_MODEL), f32)
    offset = jax.random.randint(ks[1], (BATCH, 1), 0, 1024, dtype=jnp.int32)
    positions = offset + jnp.arange(SEQ, dtype=jnp.int32)[None, :]
    col_scale = jnp.concatenate([
        jnp.ones((2 * RET_QK,), f32),
        jnp.full((RET_WIDTH,), DEEPNORM_BETA, f32),
        jnp.ones((IN_WIDTH - 2 * RET_QK - RET_WIDTH,), f32)])
    w_in = jax.random.normal(ks[2], (DEPTH, D_MODEL, IN_WIDTH), f32) * (D_MODEL ** -0.5) * col_scale
    q_norm_g = 1.0 + 0.02 * jax.random.normal(ks[3], (DEPTH, MLA_Q_RANK), f32)
    w_uq = jax.random.normal(ks[4], (DEPTH, MLA_Q_RANK, MLA_HEADS * (MLA_NOPE + MLA_ROPE)), f32) * (MLA_Q_RANK ** -0.5)
    kv_norm_g = 1.0 + 0.02 * jax.random.normal(ks[5], (DEPTH, MLA_KV_RANK), f32)
    w_uk = jax.random.normal(ks[6], (DEPTH, MLA_KV_RANK, MLA_HEADS * MLA_NOPE), f32) * (MLA_KV_RANK ** -0.5)
    w_uv = jax.random.normal(ks[7], (DEPTH, MLA_KV_RANK, MLA_HEADS * MLA_DV), f32) * (MLA_KV_RANK ** -0.5) * DEEPNORM_BETA
    ret_gn_g = 1.0 + 0.02 * jax.random.normal(ks[8], (DEPTH, RET_WIDTH), f32)
    w_out = jax.random.normal(ks[9], (DEPTH, MIX_WIDTH, D_MODEL), f32) * (MIX_WIDTH ** -0.5) * DEEPNORM_BETA
    ln1_g = 1.0 + 0.02 * jax.random.normal(ks[10], (DEPTH, D_MODEL), f32)
    ln1_b = 0.02 * jax.random.normal(ks[11], (DEPTH, D_MODEL), f32)
    w_up = jax.random.normal(ks[12], (DEPTH, D_MODEL, D_FF), f32) * (D_MODEL ** -0.5) * DEEPNORM_BETA
    w_down = jax.random.normal(ks[13], (DEPTH, D_FF, D_MODEL), f32) * (D_FF ** -0.5) * DEEPNORM_BETA
    ln2_g = 1.0 + 0.02 * jax.random.normal(ks[14], (DEPTH, D_MODEL), f32)
    ln2_b = 0.02 * jax.random.normal(ks[15], (DEPTH, D_MODEL), f32)
    return {'x': x, 'positions': positions, 'w_in': w_in, 'q_norm_g': q_norm_g, 'w_uq': w_uq,
            'kv_norm_g': kv_norm_g, 'w_uk': w_uk, 'w_uv': w_uv, 'ret_gn_g': ret_gn_g,
            'w_out': w_out, 'ln1_g': ln1_g, 'ln1_b': ln1_b, 'w_up': w_up, 'w_down': w_down,
            'ln2_g': ln2_g, 'ln2_b': ln2_b}


def reference(x, positions, w_in, q_norm_g, w_uq, kv_norm_g, w_uk, w_uv, ret_gn_g,
              w_out, ln1_g, ln1_b, w_up, w_down, ln2_g, ln2_b):
    B, S, _ = x.shape
    split_idx = [int(i) for i in np.cumsum(IN_SIZES)[:-1]]
    for l in range(DEPTH):
        proj = x @ w_in[l]
        rq, rk, rv, rg, cq, ckv, kr = jnp.split(proj, split_idx, axis=-1)
        rq = _rope(rq.reshape(B, S, RET_HEADS, RET_DK), positions)
        rk = _rope(rk.reshape(B, S, RET_HEADS, RET_DK), positions)
        ret = _retention(rq, rk, rv.reshape(B, S, RET_HEADS, RET_DV)).astype(jnp.float32)
        mu = jnp.mean(ret, axis=-1, keepdims=True)
        var = jnp.mean(jnp.square(ret - mu), axis=-1, keepdims=True)
        ret = ((ret - mu) * lax.rsqrt(var + EPS)).reshape(B, S, RET_WIDTH) * ret_gn_g[l]
        ret_out = (jax.nn.silu(rg.astype(jnp.float32)) * ret).astype(x.dtype)
        mla_out = _mla(cq, ckv, kr, positions, q_norm_g[l], w_uq[l], kv_norm_g[l], w_uk[l], w_uv[l])
        mix = jnp.concatenate([ret_out, mla_out], axis=-1) @ w_out[l]
        x = _layer_norm(DEEPNORM_ALPHA * x + mix, ln1_g[l], ln1_b[l])
        h = jnp.square(jax.nn.relu(x @ w_up[l])) @ w_down[l]
        x = _layer_norm(DEEPNORM_ALPHA * x + h, ln2_g[l], ln2_b[l])
    return x
```

```python
import functools

import jax
import jax.numpy as jnp
from jax import lax
from jax.experimental import pallas as pl
from jax.experimental.pallas import tpu as pltpu

F32 = jnp.float32
BF16 = jnp.bfloat16

LANES = 128
CHUNK = 64
RET_HEADS = 8
RET_DK = 128
RET_DV = 128
RET_QK = RET_HEADS * RET_DK
RET_WIDTH = RET_HEADS * RET_DV
MLA_HEADS = 8
MLA_NOPE = 128
MLA_ROPE = 64
MLA_DV = 128
MLA_Q_RANK = 768
MLA_KV_RANK = 512
MLA_QK_PAD = 256
MLA_WIDTH = MLA_HEADS * MLA_DV
ROPE_BASE = 10000.0
EPS = 1e-5

VMEM_LIMIT = 56 * 1024 * 1024


def _params(*sem):
    return pltpu.CompilerParams(dimension_semantics=sem, vmem_limit_bytes=VMEM_LIMIT)


def _rope_table_kernel(pos_ref, cst_ref, cr_ref, sr_ref, cm_ref, sa_ref, sb_ref):
    pos = pos_ref[...].astype(F32)
    ang_r = pos * cst_ref[0:1, :]
    cr_ref[...] = jnp.cos(ang_r)
    sr_ref[...] = jnp.sin(ang_r) * cst_ref[1:2, :]
    ang_m = pos * cst_ref[2:3, :]
    sin_m = jnp.sin(ang_m)
    cm_ref[...] = jnp.cos(ang_m) * cst_ref[3:4, :]
    sa_ref[...] = sin_m * cst_ref[4:5, :]
    sb_ref[...] = sin_m * cst_ref[5:6, :]


def _rope_tables(pos):
    t = pos.shape[0]
    tm = 2048
    half_r = RET_DK // 2
    half_m = MLA_ROPE // 2
    inv_r = ROPE_BASE ** (-jnp.arange(0, RET_DK, 2, dtype=F32) / RET_DK)
    inv_m = ROPE_BASE ** (-jnp.arange(0, MLA_ROPE, 2, dtype=F32) / MLA_ROPE)
    zeros_m = jnp.zeros((LANES - 2 * half_m,), F32)
    ones_h = jnp.ones((half_m,), F32)
    zeros_h = jnp.zeros((half_m,), F32)
    cst = jnp.stack([
        jnp.concatenate([inv_r, inv_r]),
        jnp.concatenate([-jnp.ones((half_r,), F32), jnp.ones((half_r,), F32)]),
        jnp.concatenate([inv_m, inv_m, zeros_m]),
        jnp.concatenate([ones_h, ones_h, zeros_m]),
        jnp.concatenate([-ones_h, zeros_h, zeros_m]),
        jnp.concatenate([zeros_h, ones_h, zeros_m]),
        jnp.zeros((LANES,), F32),
        jnp.zeros((LANES,), F32),
    ])
    out = jax.ShapeDtypeStruct((t, LANES), F32)
    tab_spec = pl.BlockSpec((tm, LANES), lambda i: (i, 0))
    return pl.pallas_call(
        _rope_table_kernel,
        grid=(t // tm,),
        in_specs=[pl.BlockSpec((tm, 1), lambda i: (i, 0)),
                  pl.BlockSpec((8, LANES), lambda i: (0, 0))],
        out_specs=[tab_spec] * 5,
        out_shape=[out] * 5,
        compiler_params=_params("parallel"),
        name="rope_tables",
    )(pos, cst)


def _proj_rope_kernel(x_ref, w_ref, c_ref, s_ref, o_ref, *, k_scale):
    res = jnp.dot(x_ref[...], w_ref[...], preferred_element_type=F32)
    scale = jnp.where(pl.program_id(1) == 1, k_scale, 1.0).astype(F32)
    c = c_ref[...] * scale
    s = s_ref[...] * scale
    for h in range(res.shape[1] // RET_DK):
        r = res[:, h * RET_DK:(h + 1) * RET_DK]
        o = r * c + pltpu.roll(r, RET_DK // 2, 1) * s
        o_ref[:, h * RET_DK:(h + 1) * RET_DK] = o.astype(o_ref.dtype)


def _proj_rope(xb, w, c_tab, s_tab):
    t, d = xb.shape
    n = w.shape[1]
    tm, tn = 1024, RET_QK
    return pl.pallas_call(
        functools.partial(_proj_rope_kernel, k_scale=RET_DK ** -0.5),
        grid=(t // tm, n // tn),
        in_specs=[pl.BlockSpec((tm, d), lambda i, j: (i, 0)),
                  pl.BlockSpec((d, tn), lambda i, j: (0, j)),
                  pl.BlockSpec((tm, LANES), lambda i, j: (i, 0)),
                  pl.BlockSpec((tm, LANES), lambda i, j: (i, 0))],
        out_specs=pl.BlockSpec((tm, tn), lambda i, j: (i, j)),
        out_shape=jax.ShapeDtypeStruct((t, n), BF16),
        compiler_params=_params("parallel", "arbitrary"),
        name="proj_rope",
    )(xb, w, c_tab, s_tab)


def _matmul_kernel(x_ref, w_ref, o_ref):
    o_ref[...] = jnp.dot(x_ref[...], w_ref[...], preferred_element_type=F32).astype(o_ref.dtype)


def _matmul(xb, w):
    t, d = xb.shape
    n = w.shape[1]
    tm, tn = 1024, 1024
    return pl.pallas_call(
        _matmul_kernel,
        grid=(t // tm, n // tn),
        in_specs=[pl.BlockSpec((tm, d), lambda i, j: (i, 0)),
                  pl.BlockSpec((d, tn), lambda i, j: (0, j))],
        out_specs=pl.BlockSpec((tm, tn), lambda i, j: (i, j)),
        out_shape=jax.ShapeDtypeStruct((t, n), BF16),
        compiler_params=_params("parallel", "arbitrary"),
        name="proj_plain",
    )(xb, w)


def _retention_kernel(lg_ref, q_ref, k_ref, v_ref, g_ref, gn_ref, o_ref, d_ref, *, blk):
    lg = lg_ref[pl.program_id(1)]
    row = lax.broadcasted_iota(jnp.int32, (blk, blk), 0)
    col = lax.broadcasted_iota(jnp.int32, (blk, blk), 1)
    dist = jnp.abs(row - col).astype(F32)
    d_ref[...] = jnp.where(col // CHUNK <= row // CHUNK, jnp.exp(lg * dist), 0.0)
    idx = lax.broadcasted_iota(jnp.int32, (blk, 1), 0).astype(F32)
    q_dec = jnp.exp(lg * (idx + 1.0))
    k_dec = jnp.exp(lg * (blk - 1.0 - idx))
    blk_dec = jnp.exp(jnp.full((1, 1), blk, F32) * lg)
    gn = gn_ref[...]
    nt = (((1,), (1,)), ((), ()))
    tn = (((0,), (0,)), ((), ()))

    def body(n, state):
        sl = pl.ds(pl.multiple_of(n * blk, blk), blk)
        q = q_ref[sl, :]
        k = k_ref[sl, :]
        v = v_ref[sl, :]
        s = lax.dot_general(q, k, nt, preferred_element_type=F32) * d_ref[...]
        intra = jnp.dot(s.astype(BF16), v, preferred_element_type=F32)
        cross = jnp.dot(q, state.astype(BF16), preferred_element_type=F32) * q_dec
        kd = (k.astype(F32) * k_dec).astype(BF16)
        new_state = blk_dec * state + lax.dot_general(kd, v, tn, preferred_element_type=F32)
        ret = intra + cross
        mu = jnp.mean(ret, axis=-1, keepdims=True)
        dev = ret - mu
        var = jnp.mean(dev * dev, axis=-1, keepdims=True)
        y = dev * lax.rsqrt(var + EPS) * gn
        g = g_ref[sl, :].astype(F32)
        gate = g / (1.0 + jnp.exp(-g))
        o_ref[sl, :] = (gate * y).astype(o_ref.dtype)
        return new_state

    lax.fori_loop(0, q_ref.shape[0] // blk, body, jnp.zeros((RET_DK, RET_DV), F32))


def _retention(qk, vg, gn_g, log_g, batch, seq):
    t = qk.shape[0]
    blk = 256
    h = RET_HEADS
    return pl.pallas_call(
        functools.partial(_retention_kernel, blk=blk),
        grid=(batch, h),
        in_specs=[pl.BlockSpec(memory_space=pltpu.SMEM),
                  pl.BlockSpec((seq, RET_DK), lambda b, i: (b, i)),
                  pl.BlockSpec((seq, RET_DK), lambda b, i: (b, h + i)),
                  pl.BlockSpec((seq, RET_DV), lambda b, i: (b, i)),
                  pl.BlockSpec((seq, RET_DV), lambda b, i: (b, h + i)),
                  pl.BlockSpec((1, RET_DV), lambda b, i: (0, i))],
        out_specs=pl.BlockSpec((seq, RET_DV), lambda b, i: (b, i)),
        out_shape=jax.ShapeDtypeStruct((t, RET_WIDTH), BF16),
        scratch_shapes=[pltpu.VMEM((blk, blk), F32)],
        compiler_params=_params("parallel", "arbitrary"),
        name="retention",
    )(log_g, qk, qk, vg, vg, gn_g)


def _rms(v, g):
    return v * lax.rsqrt(jnp.mean(v * v, axis=-1, keepdims=True) + EPS) * g


def _rope64(r, cm, sa, sb):
    half = MLA_ROPE // 2
    return r * cm + pltpu.roll(r, LANES - half, 1) * sa + pltpu.roll(r, half, 1) * sb


def _mla_prep_kernel(x_ref, wcq_ref, wckv_ref, wkr_ref, qg_ref, kvg_ref, wuq_ref, wuk_ref, wuv_ref,
                     cm_ref, sa_ref, sb_ref, q_ref, k_ref, v_ref):
    x = x_ref[...]
    cm = cm_ref[...]
    sa = sa_ref[...]
    sb = sb_ref[...]
    cq = jnp.dot(x, wcq_ref[...], preferred_element_type=F32)
    cqn = _rms(cq, qg_ref[...]).astype(BF16)
    q = jnp.dot(cqn, wuq_ref[...], preferred_element_type=F32)
    for h in range(MLA_HEADS):
        lo = h * MLA_QK_PAD
        q_ref[:, lo:lo + MLA_NOPE] = q[:, lo:lo + MLA_NOPE].astype(q_ref.dtype)
        q_ref[:, lo + MLA_NOPE:lo + MLA_QK_PAD] = _rope64(
            q[:, lo + MLA_NOPE:lo + MLA_QK_PAD], cm, sa, sb).astype(q_ref.dtype)
    ckv = jnp.dot(x, wckv_ref[...], preferred_element_type=F32)
    ckvn = _rms(ckv, kvg_ref[...]).astype(BF16)
    kr = jnp.dot(x, wkr_ref[...], preferred_element_type=F32)
    k_rope = _rope64(kr, cm, sa, sb).astype(k_ref.dtype)
    kn = jnp.dot(ckvn, wuk_ref[...], preferred_element_type=F32)
    for h in range(MLA_HEADS):
        lo = h * MLA_QK_PAD
        k_ref[:, lo:lo + MLA_NOPE] = kn[:, h * MLA_NOPE:(h + 1) * MLA_NOPE].astype(k_ref.dtype)
        k_ref[:, lo + MLA_NOPE:lo + MLA_QK_PAD] = k_rope
    v_ref[...] = jnp.dot(ckvn, wuv_ref[...], preferred_element_type=F32).astype(v_ref.dtype)


def _mla_prep(xb, wcq, wckv, wkr, qg, kvg, wuq, wuk, wuv, cm, sa, sb):
    t, d = xb.shape
    tm = 256
    row = lambda i: (i, 0)
    whole = lambda i: (0, 0)
    full = lambda a: pl.BlockSpec(a.shape, whole)
    qk_w = MLA_HEADS * MLA_QK_PAD
    return pl.pallas_call(
        _mla_prep_kernel,
        grid=(t // tm,),
        in_specs=[pl.BlockSpec((tm, d), row), full(wcq), full(wckv), full(wkr), full(qg), full(kvg),
                  full(wuq), full(wuk), full(wuv),
                  pl.BlockSpec((tm, LANES), row), pl.BlockSpec((tm, LANES), row),
                  pl.BlockSpec((tm, LANES), row)],
        out_specs=[pl.BlockSpec((tm, qk_w), row), pl.BlockSpec((tm, qk_w), row),
                   pl.BlockSpec((tm, MLA_WIDTH), row)],
        out_shape=[jax.ShapeDtypeStruct((t, qk_w), BF16), jax.ShapeDtypeStruct((t, qk_w), BF16),
                   jax.ShapeDtypeStruct((t, MLA_WIDTH), BF16)],
        compiler_params=_params("parallel"),
        name="mla_prep",
    )(xb, wcq, wckv, wkr, qg, kvg, wuq, wuk, wuv, cm, sa, sb)


def _mla_attn_kernel(q_ref, k_ref, v_ref, o_ref, *, blk, scale):
    qi = pl.program_id(2)
    q = q_ref[...]
    nt = (((1,), (1,)), ((), ()))

    def step(start, carry, masked):
        m, l, acc = carry
        sl = pl.ds(pl.multiple_of(start, blk), blk)
        s = lax.dot_general(q, k_ref[sl, :], nt, preferred_element_type=F32) * scale
        if masked:
            row = lax.broadcasted_iota(jnp.int32, (blk, blk), 0)
            col = lax.broadcasted_iota(jnp.int32, (blk, blk), 1)
            s = jnp.where(col // CHUNK <= row // CHUNK, s, -jnp.inf)
        m_new = jnp.maximum(m, jnp.max(s, axis=-1, keepdims=True))
        alpha = jnp.exp(m - m_new)
        p = jnp.exp(s - m_new)
        l_new = alpha * l + jnp.sum(p, axis=-1, keepdims=True)
        acc_new = alpha * acc + jnp.dot(p.astype(BF16), v_ref[sl, :], preferred_element_type=F32)
        return m_new, l_new, acc_new

    init = (jnp.full((blk, 1), -jnp.inf, F32), jnp.zeros((blk, 1), F32),
            jnp.zeros((blk, MLA_DV), F32))
    carry = lax.fori_loop(0, qi, lambda j, c: step(j * blk, c, False), init)
    m, l, acc = step(qi * blk, carry, True)
    o_ref[...] = (acc / l).astype(o_ref.dtype)


def _mla_attn(q, k, v, batch, seq):
    t = q.shape[0]
    blk = 512
    nq = seq // blk
    return pl.pallas_call(
        functools.partial(_mla_attn_kernel, blk=blk, scale=(MLA_NOPE + MLA_ROPE) ** -0.5),
        grid=(batch, MLA_HEADS, nq),
        in_specs=[pl.BlockSpec((blk, MLA_QK_PAD), lambda b, h, i: (b * nq + i, h)),
                  pl.BlockSpec((seq, MLA_QK_PAD), lambda b, h, i: (b, h)),
                  pl.BlockSpec((seq, MLA_DV), lambda b, h, i: (b, h))],
        out_specs=pl.BlockSpec((blk, MLA_DV), lambda b, h, i: (b * nq + i, h)),
        out_shape=jax.ShapeDtypeStruct((t, MLA_WIDTH), BF16),
        compiler_params=_params("parallel", "parallel", "arbitrary"),
        name="mla_attn",
    )(q, k, v)


def _layer_norm(y, g, b):
    mu = jnp.mean(y, axis=-1, keepdims=True)
    dev = y - mu
    var = jnp.mean(dev * dev, axis=-1, keepdims=True)
    return dev * lax.rsqrt(var + EPS) * g + b


def _out_ln_kernel(ret_ref, mla_ref, x_ref, w_ref, g_ref, b_ref, o_ref, *, alpha):
    kr = ret_ref.shape[1]
    mix = jnp.dot(ret_ref[...], w_ref[0:kr, :], preferred_element_type=F32)
    mix = mix + jnp.dot(mla_ref[...], w_ref[kr:, :], preferred_element_type=F32)
    y = alpha * x_ref[...] + mix
    o_ref[...] = _layer_norm(y, g_ref[...], b_ref[...])


def _out_ln(ret, mla, xf, w, g, b, alpha):
    t, d = xf.shape
    tm = 512
    row = lambda i: (i, 0)
    whole = lambda i: (0, 0)
    return pl.pallas_call(
        functools.partial(_out_ln_kernel, alpha=alpha),
        grid=(t // tm,),
        in_specs=[pl.BlockSpec((tm, ret.shape[1]), row), pl.BlockSpec((tm, mla.shape[1]), row),
                  pl.BlockSpec((tm, d), row), pl.BlockSpec(w.shape, whole),
                  pl.BlockSpec((1, d), whole), pl.BlockSpec((1, d), whole)],
        out_specs=pl.BlockSpec((tm, d), row),
        out_shape=jax.ShapeDtypeStruct((t, d), F32),
        compiler_params=_params("parallel"),
        name="out_ln",
    )(ret, mla, xf, w, g, b)


def _ffn_ln_kernel(x_ref, wu_ref, wd_ref, g_ref, b_ref, o_ref, xb_ref, acc_ref, *, alpha):
    f = pl.program_id(1)

    @pl.when(f == 0)
    def _():
        xb_ref[...] = x_ref[...].astype(BF16)
        acc_ref[...] = jnp.zeros_like(acc_ref)

    u = jnp.dot(xb_ref[...], wu_ref[...], preferred_element_type=F32)
    a = jnp.square(jnp.maximum(u, 0.0)).astype(BF16)
    acc_ref[...] += jnp.dot(a, wd_ref[...], preferred_element_type=F32)

    @pl.when(f == pl.num_programs(1) - 1)
    def _():
        y = alpha * x_ref[...] + acc_ref[...]
        o_ref[...] = _layer_norm(y, g_ref[...], b_ref[...])


def _ffn_ln(x1, wu, wd, g, b, alpha):
    t, d = x1.shape
    dff = wu.shape[1]
    tm, tf = 512, 1024
    return pl.pallas_call(
        functools.partial(_ffn_ln_kernel, alpha=alpha),
        grid=(t // tm, dff // tf),
        in_specs=[pl.BlockSpec((tm, d), lambda i, f: (i, 0)),
                  pl.BlockSpec((d, tf), lambda i, f: (0, f)),
                  pl.BlockSpec((tf, d), lambda i, f: (f, 0)),
                  pl.BlockSpec((1, d), lambda i, f: (0, 0)),
                  pl.BlockSpec((1, d), lambda i, f: (0, 0))],
        out_specs=pl.BlockSpec((tm, d), lambda i, f: (i, 0)),
        out_shape=jax.ShapeDtypeStruct((t, d), F32),
        scratch_shapes=[pltpu.VMEM((tm, d), BF16), pltpu.VMEM((tm, d), F32)],
        compiler_params=_params("parallel", "arbitrary"),
        name="ffn_ln",
    )(x1, wu, wd, g, b)


def kernel(x, positions, w_in, q_norm_g, w_uq, kv_norm_g, w_uk, w_uv, ret_gn_g, w_out, ln1_g, ln1_b,
           w_up, w_down, ln2_g, ln2_b):
    batch, seq, d = x.shape
    depth = w_in.shape[0]
    t = batch * seq
    alpha = (2.0 * depth) ** 0.25
    xf = x.reshape(t, d)
    c_ret, s_ret, c_mla, sa_mla, sb_mla = _rope_tables(positions.reshape(t, 1))
    log_g = jnp.log1p(-jnp.exp2(-5.0 - jnp.arange(RET_HEADS, dtype=F32)))
    o_cq = 2 * RET_QK + 2 * RET_WIDTH
    o_ckv = o_cq + MLA_Q_RANK
    o_kr = o_ckv + MLA_KV_RANK
    for l in range(depth):
        w = w_in[l]
        w_qk = w[:, :2 * RET_QK].astype(BF16)
        w_vg = w[:, 2 * RET_QK:o_cq].astype(BF16)
        w_cq = w[:, o_cq:o_ckv].astype(BF16)
        w_ckv = w[:, o_ckv:o_kr].astype(BF16)
        w_kr = jnp.pad(w[:, o_kr:], ((0, 0), (0, LANES - MLA_ROPE))).astype(BF16)
        wuq = w_uq[l].reshape(MLA_Q_RANK, MLA_HEADS, MLA_NOPE + MLA_ROPE)
        wuq = jnp.pad(wuq, ((0, 0), (0, 0), (0, MLA_QK_PAD - MLA_NOPE - MLA_ROPE)))
        wuq = wuq.reshape(MLA_Q_RANK, MLA_HEADS * MLA_QK_PAD).astype(BF16)

        xb = xf.astype(BF16)
        qk = _proj_rope(xb, w_qk, c_ret, s_ret)
        vg = _matmul(xb, w_vg)
        ret = _retention(qk, vg, ret_gn_g[l].reshape(1, RET_WIDTH), log_g, batch, seq)
        q, k, v = _mla_prep(xb, w_cq, w_ckv, w_kr, q_norm_g[l].reshape(1, -1),
                            kv_norm_g[l].reshape(1, -1), wuq, w_uk[l].astype(BF16),
                            w_uv[l].astype(BF16), c_mla, sa_mla, sb_mla)
        mla = _mla_attn(q, k, v, batch, seq)
        x1 = _out_ln(ret, mla, xf, w_out[l].astype(BF16), ln1_g[l].reshape(1, d),
                     ln1_b[l].reshape(1, d), alpha)
        xf = _ffn_ln(x1, w_up[l].astype(BF16), w_down[l].astype(BF16), ln2_g[l].reshape(1, d),
                     ln2_b[l].reshape(1, d), alpha)
    return xf.reshape(batch, seq, d)
```

```python
import functools

import jax
import jax.numpy as jnp
from jax import lax
from jax.experimental import pallas as pl
from jax.experimental.pallas import tpu as pltpu

F32 = jnp.float32
BF16 = jnp.bfloat16

LANES = 128
CHUNK = 64
RET_HEADS = 8
RET_DK = 128
RET_DV = 128
RET_QK = RET_HEADS * RET_DK
RET_WIDTH = RET_HEADS * RET_DV
MLA_HEADS = 8
MLA_NOPE = 128
MLA_ROPE = 64
MLA_DV = 128
MLA_Q_RANK = 768
MLA_KV_RANK = 512
MLA_QK_PAD = 256
MLA_WIDTH = MLA_HEADS * MLA_DV
MLA_Q_SCALE = (MLA_NOPE + MLA_ROPE) ** -0.5 * 1.4426950408889634
ROPE_BASE = 10000.0
EPS = 1e-5

VMEM_LIMIT = 56 * 1024 * 1024


def _params(*sem):
    return pltpu.CompilerParams(dimension_semantics=sem, vmem_limit_bytes=VMEM_LIMIT)


def _rope_table_kernel(pos_ref, cst_ref, cr_ref, sr_ref, cm_ref, sa_ref, sb_ref):
    pos = pos_ref[...].astype(F32)
    ang = pos * cst_ref[0:1, :]
    c = jnp.cos(ang)
    s = jnp.sin(ang)
    half = LANES // 2
    quarter = LANES // 4
    c64 = pltpu.roll(c, half, 1)
    s64 = pltpu.roll(s, half, 1)
    cr_ref[...] = c * cst_ref[1:2, :] + c64 * cst_ref[3:4, :]
    sr_ref[...] = s * cst_ref[2:3, :] + s64 * cst_ref[3:4, :]
    cm_ref[...] = c64 * cst_ref[4:5, :] + pltpu.roll(c, half + quarter, 1) * cst_ref[6:7, :]
    sa_ref[...] = s64 * cst_ref[5:6, :]
    sb_ref[...] = pltpu.roll(s, half + quarter, 1) * cst_ref[6:7, :]


def _rope_tables(pos):
    t = pos.shape[0]
    tm = 2048
    half_r = RET_DK // 2
    half_m = MLA_ROPE // 2
    inv_r = ROPE_BASE ** (-jnp.arange(0, RET_DK, 2, dtype=F32) / RET_DK)
    inv_m = ROPE_BASE ** (-jnp.arange(0, MLA_ROPE, 2, dtype=F32) / MLA_ROPE)
    assert half_r == LANES // 2 and half_m == LANES // 4
    lane = jnp.arange(LANES)
    lo64 = (lane < half_r).astype(F32)
    q0 = (lane < half_m).astype(F32)
    q1 = ((lane >= half_m) & (lane < 2 * half_m)).astype(F32)
    cst = jnp.stack([
        jnp.concatenate([inv_r, inv_m, jnp.zeros((LANES - half_r - half_m,), F32)]),
        lo64, -lo64, 1.0 - lo64, q0, -q0, q1, jnp.zeros((LANES,), F32)])
    out = jax.ShapeDtypeStruct((t, LANES), F32)
    tab_spec = pl.BlockSpec((tm, LANES), lambda i: (i, 0))
    return pl.pallas_call(
        _rope_table_kernel,
        grid=(t // tm,),
        in_specs=[pl.BlockSpec((tm, 1), lambda i: (i, 0)),
                  pl.BlockSpec((8, LANES), lambda i: (0, 0))],
        out_specs=[tab_spec] * 5,
        out_shape=[out] * 5,
        compiler_params=_params("parallel"),
        name="rope_tables",
    )(pos, cst)


def _proj_rope_kernel(x_ref, w_ref, c_ref, s_ref, o_ref, *, k_scale):
    res = jnp.dot(x_ref[...], w_ref[...], preferred_element_type=F32)
    scale = jnp.where(pl.program_id(1) == 1, k_scale, 1.0).astype(F32)
    c = c_ref[...] * scale
    s = s_ref[...] * scale
    for h in range(res.shape[1] // RET_DK):
        r = res[:, h * RET_DK:(h + 1) * RET_DK]
        o = r * c + pltpu.roll(r, RET_DK // 2, 1) * s
        o_ref[:, h * RET_DK:(h + 1) * RET_DK] = o.astype(o_ref.dtype)


def _proj_rope(xb, w, c_tab, s_tab):
    t, d = xb.shape
    n = w.shape[1]
    tm, tn = 1024, RET_QK
    return pl.pallas_call(
        functools.partial(_proj_rope_kernel, k_scale=RET_DK ** -0.5),
        grid=(t // tm, n // tn),
        in_specs=[pl.BlockSpec((tm, d), lambda i, j: (i, 0)),
                  pl.BlockSpec((d, tn), lambda i, j: (0, j)),
                  pl.BlockSpec((tm, LANES), lambda i, j: (i, 0)),
                  pl.BlockSpec((tm, LANES), lambda i, j: (i, 0))],
        out_specs=pl.BlockSpec((tm, tn), lambda i, j: (i, j)),
        out_shape=jax.ShapeDtypeStruct((t, n), BF16),
        compiler_params=_params("parallel", "arbitrary"),
        name="proj_rope",
    )(xb, w, c_tab, s_tab)


def _matmul_kernel(x_ref, w_ref, o_ref):
    o_ref[...] = jnp.dot(x_ref[...], w_ref[...], preferred_element_type=F32).astype(o_ref.dtype)


def _matmul(xb, w):
    t, d = xb.shape
    n = w.shape[1]
    tm, tn = 1024, 1024
    return pl.pallas_call(
        _matmul_kernel,
        grid=(t // tm, n // tn),
        in_specs=[pl.BlockSpec((tm, d), lambda i, j: (i, 0)),
                  pl.BlockSpec((d, tn), lambda i, j: (0, j))],
        out_specs=pl.BlockSpec((tm, tn), lambda i, j: (i, j)),
        out_shape=jax.ShapeDtypeStruct((t, n), BF16),
        compiler_params=_params("parallel", "arbitrary"),
        name="proj_plain",
    )(xb, w)


def _retention_kernel(lg_ref, q_ref, k_ref, v_ref, g_ref, gn_ref, o_ref, d_ref, *, blk):
    lg = lg_ref[pl.program_id(1)]
    row = lax.broadcasted_iota(jnp.int32, (blk, blk), 0)
    col = lax.broadcasted_iota(jnp.int32, (blk, blk), 1)
    dist = jnp.abs(row - col).astype(F32)
    d_ref[...] = jnp.where(col // CHUNK <= row // CHUNK, jnp.exp(lg * dist), 0.0)
    idx = lax.broadcasted_iota(jnp.int32, (blk, 1), 0).astype(F32)
    q_dec = jnp.exp(lg * (idx + 1.0))
    k_dec = jnp.exp(lg * (blk - 1.0 - idx))
    blk_dec = jnp.exp(jnp.full((1, 1), blk, F32) * lg)
    gn = gn_ref[...]
    nt = (((1,), (1,)), ((), ()))
    tn = (((0,), (0,)), ((), ()))

    def body(n, state):
        sl = pl.ds(pl.multiple_of(n * blk, blk), blk)
        q = q_ref[sl, :]
        k = k_ref[sl, :]
        v = v_ref[sl, :]
        s = lax.dot_general(q, k, nt, preferred_element_type=F32) * d_ref[...]
        intra = jnp.dot(s.astype(BF16), v, preferred_element_type=F32)
        cross = jnp.dot(q, state.astype(BF16), preferred_element_type=F32) * q_dec
        kd = (k.astype(F32) * k_dec).astype(BF16)
        new_state = blk_dec * state + lax.dot_general(kd, v, tn, preferred_element_type=F32)
        ret = intra + cross
        mu = jnp.mean(ret, axis=-1, keepdims=True)
        dev = ret - mu
        var = jnp.mean(dev * dev, axis=-1, keepdims=True)
        y = dev * lax.rsqrt(var + EPS) * gn
        g = g_ref[sl, :].astype(F32)
        gate = g / (1.0 + jnp.exp(-g))
        o_ref[sl, :] = (gate * y).astype(o_ref.dtype)
        return new_state

    lax.fori_loop(0, q_ref.shape[0] // blk, body, jnp.zeros((RET_DK, RET_DV), F32), unroll=4)


def _retention(qk, vg, gn_g, log_g, batch, seq):
    t = qk.shape[0]
    blk = 256
    h = RET_HEADS
    return pl.pallas_call(
        functools.partial(_retention_kernel, blk=blk),
        grid=(batch, h),
        in_specs=[pl.BlockSpec(memory_space=pltpu.SMEM),
                  pl.BlockSpec((seq, RET_DK), lambda b, i: (b, i)),
                  pl.BlockSpec((seq, RET_DK), lambda b, i: (b, h + i)),
                  pl.BlockSpec((seq, RET_DV), lambda b, i: (b, i)),
                  pl.BlockSpec((seq, RET_DV), lambda b, i: (b, h + i)),
                  pl.BlockSpec((1, RET_DV), lambda b, i: (0, i))],
        out_specs=pl.BlockSpec((seq, RET_DV), lambda b, i: (b, i)),
        out_shape=jax.ShapeDtypeStruct((t, RET_WIDTH), BF16),
        scratch_shapes=[pltpu.VMEM((blk, blk), F32)],
        compiler_params=_params("parallel", "arbitrary"),
        name="retention",
    )(log_g, qk, qk, vg, vg, gn_g)


def _rms(v, g):
    return v * lax.rsqrt(jnp.mean(v * v, axis=-1, keepdims=True) + EPS) * g


def _rope64(r, cm, sa, sb):
    half = MLA_ROPE // 2
    return r * cm + pltpu.roll(r, LANES - half, 1) * sa + pltpu.roll(r, half, 1) * sb


def _mla_prep_kernel(x_ref, wcq_ref, wckv_ref, wkr_ref, qg_ref, kvg_ref, wuq_ref, wuk_ref, wuv_ref,
                     cm_ref, sa_ref, sb_ref, q_ref, k_ref, v_ref):
    x = x_ref[...]
    cm = cm_ref[...]
    sa = sa_ref[...]
    sb = sb_ref[...]
    cq = jnp.dot(x, wcq_ref[...], preferred_element_type=F32)
    cqn = _rms(cq, qg_ref[...]).astype(BF16)
    q = jnp.dot(cqn, wuq_ref[...], preferred_element_type=F32) * MLA_Q_SCALE
    for h in range(MLA_HEADS):
        lo = h * MLA_QK_PAD
        q_ref[:, lo:lo + MLA_NOPE] = q[:, lo:lo + MLA_NOPE].astype(q_ref.dtype)
        q_ref[:, lo + MLA_NOPE:lo + MLA_QK_PAD] = _rope64(
            q[:, lo + MLA_NOPE:lo + MLA_QK_PAD], cm, sa, sb).astype(q_ref.dtype)
    ckv = jnp.dot(x, wckv_ref[...], preferred_element_type=F32)
    ckvn = _rms(ckv, kvg_ref[...]).astype(BF16)
    kr = jnp.dot(x, wkr_ref[...], preferred_element_type=F32)
    k_rope = _rope64(kr, cm, sa, sb).astype(k_ref.dtype)
    kn = jnp.dot(ckvn, wuk_ref[...], preferred_element_type=F32)
    for h in range(MLA_HEADS):
        lo = h * MLA_QK_PAD
        k_ref[:, lo:lo + MLA_NOPE] = kn[:, h * MLA_NOPE:(h + 1) * MLA_NOPE].astype(k_ref.dtype)
        k_ref[:, lo + MLA_NOPE:lo + MLA_QK_PAD] = k_rope
    v_ref[...] = jnp.dot(ckvn, wuv_ref[...], preferred_element_type=F32).astype(v_ref.dtype)


def _mla_prep(xb, wcq, wckv, wkr, qg, kvg, wuq, wuk, wuv, cm, sa, sb):
    t, d = xb.shape
    tm = 256
    row = lambda i: (i, 0)
    whole = lambda i: (0, 0)
    full = lambda a: pl.BlockSpec(a.shape, whole)
    qk_w = MLA_HEADS * MLA_QK_PAD
    return pl.pallas_call(
        _mla_prep_kernel,
        grid=(t // tm,),
        in_specs=[pl.BlockSpec((tm, d), row), full(wcq), full(wckv), full(wkr), full(qg), full(kvg),
                  full(wuq), full(wuk), full(wuv),
                  pl.BlockSpec((tm, LANES), row), pl.BlockSpec((tm, LANES), row),
                  pl.BlockSpec((tm, LANES), row)],
        out_specs=[pl.BlockSpec((tm, qk_w), row), pl.BlockSpec((tm, qk_w), row),
                   pl.BlockSpec((tm, MLA_WIDTH), row)],
        out_shape=[jax.ShapeDtypeStruct((t, qk_w), BF16), jax.ShapeDtypeStruct((t, qk_w), BF16),
                   jax.ShapeDtypeStruct((t, MLA_WIDTH), BF16)],
        compiler_params=_params("parallel"),
        name="mla_prep",
    )(xb, wcq, wckv, wkr, qg, kvg, wuq, wuk, wuv, cm, sa, sb)


def _mla_attn_kernel(q_ref, k_ref, v_ref, o_ref, sa_ref, sb_ref, *, kblk):
    qi = pl.program_id(2)
    qblk = q_ref.shape[0]
    q = q_ref[...]
    nt = (((1,), (1,)), ((), ()))
    tn = (((0,), (0,)), ((), ()))

    def keys(j):
        return pl.ds(pl.multiple_of(j * kblk, kblk), kblk)

    def scores(s_ref, j):
        s_ref[...] = lax.dot_general(k_ref[keys(j), :], q, nt, preferred_element_type=F32)

    def update(s_ref, j, carry, key_off=None):
        m, l, acc = carry
        st = s_ref[...]
        if key_off is not None:
            key = lax.broadcasted_iota(jnp.int32, st.shape, 0) + key_off
            qry = lax.broadcasted_iota(jnp.int32, st.shape, 1)
            st = jnp.where(key // CHUNK <= qry // CHUNK, st, -jnp.inf)
        m_new = jnp.maximum(m, jnp.max(st, axis=0, keepdims=True))
        alpha = jnp.exp2(m - m_new)
        p = jnp.exp2(st - m_new)
        l_new = alpha * l + jnp.sum(p, axis=0, keepdims=True)
        pv = lax.dot_general(v_ref[keys(j), :], p.astype(BF16), tn, preferred_element_type=F32)
        return m_new, l_new, alpha * acc + pv

    def body(jj, carry):
        j = 2 * jj
        scores(sb_ref, j + 1)
        carry = update(sa_ref, j, carry)
        scores(sa_ref, j + 2)
        return update(sb_ref, j + 1, carry)

    init = (jnp.full((1, qblk), -jnp.inf, F32), jnp.zeros((1, qblk), F32),
            jnp.zeros((MLA_DV, qblk), F32))
    scores(sa_ref, 0)
    nfull = qi * (qblk // kblk)
    carry = lax.fori_loop(0, nfull // 2, body, init)
    scores(sb_ref, nfull + 1)
    carry = update(sa_ref, nfull, carry, key_off=0)
    m, l, acc = update(sb_ref, nfull + 1, carry, key_off=kblk)
    o_ref[...] = (acc / l).T.astype(o_ref.dtype)


def _mla_attn(q, k, v, batch, seq):
    t = q.shape[0]
    qblk = 512
    kblk = qblk // 2
    nq = seq // qblk
    return pl.pallas_call(
        functools.partial(_mla_attn_kernel, kblk=kblk),
        grid=(batch, MLA_HEADS, nq),
        in_specs=[pl.BlockSpec((qblk, MLA_QK_PAD), lambda b, h, i: (b * nq + i, h)),
                  pl.BlockSpec((seq, MLA_QK_PAD), lambda b, h, i: (b, h)),
                  pl.BlockSpec((seq, MLA_DV), lambda b, h, i: (b, h))],
        out_specs=pl.BlockSpec((qblk, MLA_DV), lambda b, h, i: (b * nq + i, h)),
        out_shape=jax.ShapeDtypeStruct((t, MLA_WIDTH), BF16),
        scratch_shapes=[pltpu.VMEM((kblk, qblk), F32), pltpu.VMEM((kblk, qblk), F32)],
        compiler_params=_params("parallel", "parallel", "arbitrary"),
        name="mla_attn",
    )(q, k, v)


def _layer_norm(y, g, b):
    mu = jnp.mean(y, axis=-1, keepdims=True)
    dev = y - mu
    var = jnp.mean(dev * dev, axis=-1, keepdims=True)
    return dev * lax.rsqrt(var + EPS) * g + b


def _out_ln_kernel(ret_ref, mla_ref, x_ref, w_ref, g_ref, b_ref, o_ref, *, alpha):
    kr = ret_ref.shape[1]
    mix = jnp.dot(ret_ref[...], w_ref[0:kr, :], preferred_element_type=F32)
    mix = mix + jnp.dot(mla_ref[...], w_ref[kr:, :], preferred_element_type=F32)
    y = alpha * x_ref[...] + mix
    o_ref[...] = _layer_norm(y, g_ref[...], b_ref[...])


def _out_ln(ret, mla, xf, w, g, b, alpha):
    t, d = xf.shape
    tm = 512
    row = lambda i: (i, 0)
    whole = lambda i: (0, 0)
    return pl.pallas_call(
        functools.partial(_out_ln_kernel, alpha=alpha),
        grid=(t // tm,),
        in_specs=[pl.BlockSpec((tm, ret.shape[1]), row), pl.BlockSpec((tm, mla.shape[1]), row),
                  pl.BlockSpec((tm, d), row), pl.BlockSpec(w.shape, whole),
                  pl.BlockSpec((1, d), whole), pl.BlockSpec((1, d), whole)],
        out_specs=pl.BlockSpec((tm, d), row),
        out_shape=jax.ShapeDtypeStruct((t, d), F32),
        compiler_params=_params("parallel"),
        name="out_ln",
    )(ret, mla, xf, w, g, b)


def _ffn_ln_kernel(x_ref, wu_ref, wd_ref, g_ref, b_ref, o_ref, xb_ref, acc_ref, *, alpha):
    f = pl.program_id(1)

    @pl.when(f == 0)
    def _():
        xb_ref[...] = x_ref[...].astype(BF16)
        acc_ref[...] = jnp.zeros_like(acc_ref)

    u = jnp.dot(xb_ref[...], wu_ref[...], preferred_element_type=F32)
    a = jnp.square(jnp.maximum(u, 0.0)).astype(BF16)
    acc_ref[...] += jnp.dot(a, wd_ref[...], preferred_element_type=F32)

    @pl.when(f == pl.num_programs(1) - 1)
    def _():
        y = alpha * x_ref[...] + acc_ref[...]
        o_ref[...] = _layer_norm(y, g_ref[...], b_ref[...])


def _ffn_ln(x1, wu, wd, g, b, alpha):
    t, d = x1.shape
    dff = wu.shape[1]
    tm, tf = 512, 1024
    return pl.pallas_call(
        functools.partial(_ffn_ln_kernel, alpha=alpha),
        grid=(t // tm, dff // tf),
        in_specs=[pl.BlockSpec((tm, d), lambda i, f: (i, 0)),
                  pl.BlockSpec((d, tf), lambda i, f: (0, f)),
                  pl.BlockSpec((tf, d), lambda i, f: (f, 0)),
                  pl.BlockSpec((1, d), lambda i, f: (0, 0)),
                  pl.BlockSpec((1, d), lambda i, f: (0, 0))],
        out_specs=pl.BlockSpec((tm, d), lambda i, f: (i, 0)),
        out_shape=jax.ShapeDtypeStruct((t, d), F32),
        scratch_shapes=[pltpu.VMEM((tm, d), BF16), pltpu.VMEM((tm, d), F32)],
        compiler_params=_params("parallel", "arbitrary"),
        name="ffn_ln",
    )(x1, wu, wd, g, b)


def kernel(x, positions, w_in, q_norm_g, w_uq, kv_norm_g, w_uk, w_uv, ret_gn_g, w_out, ln1_g, ln1_b,
           w_up, w_down, ln2_g, ln2_b):
    batch, seq, d = x.shape
    depth = w_in.shape[0]
    t = batch * seq
    alpha = (2.0 * depth) ** 0.25
    xf = x.reshape(t, d)
    c_ret, s_ret, c_mla, sa_mla, sb_mla = _rope_tables(positions.reshape(t, 1))
    log_g = jnp.log1p(-jnp.exp2(-5.0 - jnp.arange(RET_HEADS, dtype=F32)))
    o_cq = 2 * RET_QK + 2 * RET_WIDTH
    o_ckv = o_cq + MLA_Q_RANK
    o_kr = o_ckv + MLA_KV_RANK
    for l in range(depth):
        w = w_in[l]
        w_qk = w[:, :2 * RET_QK].astype(BF16)
        w_vg = w[:, 2 * RET_QK:o_cq].astype(BF16)
        w_cq = w[:, o_cq:o_ckv].astype(BF16)
        w_ckv = w[:, o_ckv:o_kr].astype(BF16)
        w_kr = jnp.pad(w[:, o_kr:], ((0, 0), (0, LANES - MLA_ROPE))).astype(BF16)
        wuq = w_uq[l].reshape(MLA_Q_RANK, MLA_HEADS, MLA_NOPE + MLA_ROPE)
        wuq = jnp.pad(wuq, ((0, 0), (0, 0), (0, MLA_QK_PAD - MLA_NOPE - MLA_ROPE)))
        wuq = wuq.reshape(MLA_Q_RANK, MLA_HEADS * MLA_QK_PAD).astype(BF16)

        xb = xf.astype(BF16)
        qk = _proj_rope(xb, w_qk, c_ret, s_ret)
        vg = _matmul(xb, w_vg)
        ret = _retention(qk, vg, ret_gn_g[l].reshape(1, RET_WIDTH), log_g, batch, seq)
        q, k, v = _mla_prep(xb, w_cq, w_ckv, w_kr, q_norm_g[l].reshape(1, -1),
                            kv_norm_g[l].reshape(1, -1), wuq, w_uk[l].astype(BF16),
                            w_uv[l].astype(BF16), c_mla, sa_mla, sb_mla)
        mla = _mla_attn(q, k, v, batch, seq)
        x1 = _out_ln(ret, mla, xf, w_out[l].astype(BF16), ln1_g[l].reshape(1, d),
                     ln1_b[l].reshape(1, d), alpha)
        xf = _ffn_ln(x1, w_up[l].astype(BF16), w_down[l].astype(BF16), ln2_g[l].reshape(1, d),
                     ln2_b[l].reshape(1, d), alpha)
    return xf.reshape(batch, seq, d)
```

```python
import functools

import jax
import jax.numpy as jnp
from jax import lax
from jax.experimental import pallas as pl
from jax.experimental.pallas import tpu as pltpu

F32 = jnp.float32
BF16 = jnp.bfloat16

LANES = 128
CHUNK = 64
RET_HEADS = 8
RET_DK = 128
RET_DV = 128
RET_QK = RET_HEADS * RET_DK
RET_WIDTH = RET_HEADS * RET_DV
MLA_HEADS = 8
MLA_NOPE = 128
MLA_ROPE = 64
MLA_DV = 128
MLA_Q_RANK = 768
MLA_KV_RANK = 512
MLA_QK_PAD = 256
MLA_WIDTH = MLA_HEADS * MLA_DV
MLA_Q_SCALE = (MLA_NOPE + MLA_ROPE) ** -0.5 * 1.4426950408889634
ROPE_BASE = 10000.0
EPS = 1e-5

VMEM_LIMIT = 56 * 1024 * 1024
OUT_LN_SPLIT = 2


def _params(*sem):
    return pltpu.CompilerParams(dimension_semantics=sem, vmem_limit_bytes=VMEM_LIMIT)


def _rope_table_kernel(pos_ref, cst_ref, cr_ref, sr_ref, cm_ref, sa_ref, sb_ref):
    pos = pos_ref[...].astype(F32)
    ang = pos * cst_ref[0:1, :]
    c = jnp.cos(ang)
    s = jnp.sin(ang)
    half = LANES // 2
    quarter = LANES // 4
    c64 = pltpu.roll(c, half, 1)
    s64 = pltpu.roll(s, half, 1)
    cr_ref[...] = c * cst_ref[1:2, :] + c64 * cst_ref[3:4, :]
    sr_ref[...] = s * cst_ref[2:3, :] + s64 * cst_ref[3:4, :]
    cm_ref[...] = c64 * cst_ref[4:5, :] + pltpu.roll(c, half + quarter, 1) * cst_ref[6:7, :]
    sa_ref[...] = s64 * cst_ref[5:6, :]
    sb_ref[...] = pltpu.roll(s, half + quarter, 1) * cst_ref[6:7, :]


def _rope_tables(pos):
    t = pos.shape[0]
    tm = 2048
    half_r = RET_DK // 2
    half_m = MLA_ROPE // 2
    inv_r = ROPE_BASE ** (-jnp.arange(0, RET_DK, 2, dtype=F32) / RET_DK)
    inv_m = ROPE_BASE ** (-jnp.arange(0, MLA_ROPE, 2, dtype=F32) / MLA_ROPE)
    assert half_r == LANES // 2 and half_m == LANES // 4
    lane = jnp.arange(LANES)
    lo64 = (lane < half_r).astype(F32)
    q0 = (lane < half_m).astype(F32)
    q1 = ((lane >= half_m) & (lane < 2 * half_m)).astype(F32)
    cst = jnp.stack([
        jnp.concatenate([inv_r, inv_m, jnp.zeros((LANES - half_r - half_m,), F32)]),
        lo64, -lo64, 1.0 - lo64, q0, -q0, q1, jnp.zeros((LANES,), F32)])
    out = jax.ShapeDtypeStruct((t, LANES), F32)
    tab_spec = pl.BlockSpec((tm, LANES), lambda i: (i, 0))
    return pl.pallas_call(
        _rope_table_kernel,
        grid=(t // tm,),
        in_specs=[pl.BlockSpec((tm, 1), lambda i: (i, 0)),
                  pl.BlockSpec((8, LANES), lambda i: (0, 0))],
        out_specs=[tab_spec] * 5,
        out_shape=[out] * 5,
        compiler_params=_params("parallel"),
        name="rope_tables",
    )(pos, cst)


def _proj_kernel(x_ref, w_ref, c_ref, s_ref, o_ref, xb_ref, *, k_scale):
    j = pl.program_id(1)

    @pl.when(j == 0)
    def _():
        xb_ref[...] = x_ref[...].astype(xb_ref.dtype)

    res = jnp.dot(xb_ref[...], w_ref[...], preferred_element_type=F32)
    rotary = j < 2
    scale = jnp.where(j == 1, k_scale, 1.0).astype(F32)
    c = jnp.where(rotary, c_ref[...] * scale, 1.0)
    s = jnp.where(rotary, s_ref[...] * scale, 0.0)
    for h in range(res.shape[1] // RET_DK):
        r = res[:, h * RET_DK:(h + 1) * RET_DK]
        o = r * c + pltpu.roll(r, RET_DK // 2, 1) * s
        o_ref[:, h * RET_DK:(h + 1) * RET_DK] = o.astype(o_ref.dtype)


def _proj(xf, w, c_tab, s_tab):
    t, d = xf.shape
    n = w.shape[1]
    tm, tn = 1024, RET_QK
    assert RET_QK == RET_WIDTH and n == 4 * tn
    return pl.pallas_call(
        functools.partial(_proj_kernel, k_scale=RET_DK ** -0.5),
        grid=(t // tm, n // tn),
        in_specs=[pl.BlockSpec((tm, d), lambda i, j: (i, 0)),
                  pl.BlockSpec((d, tn), lambda i, j: (0, j)),
                  pl.BlockSpec((tm, LANES), lambda i, j: (i, 0)),
                  pl.BlockSpec((tm, LANES), lambda i, j: (i, 0))],
        out_specs=[pl.BlockSpec((tm, tn), lambda i, j: (i, j)),
                   pl.BlockSpec((tm, d), lambda i, j: (i, 0))],
        out_shape=[jax.ShapeDtypeStruct((t, n), BF16), jax.ShapeDtypeStruct((t, d), BF16)],
        compiler_params=_params("parallel", "arbitrary"),
        name="proj",
    )(xf, w, c_tab, s_tab)


def _retention_kernel(lg_ref, q_ref, k_ref, v_ref, g_ref, gn_ref, o_ref, d_ref, *, blk):
    lg = lg_ref[pl.program_id(1)]
    row = lax.broadcasted_iota(jnp.int32, (blk, blk), 0)
    col = lax.broadcasted_iota(jnp.int32, (blk, blk), 1)
    dist = jnp.abs(row - col).astype(F32)
    d_ref[...] = jnp.where(col // CHUNK <= row // CHUNK, jnp.exp(lg * dist), 0.0)
    idx = lax.broadcasted_iota(jnp.int32, (blk, 1), 0).astype(F32)
    q_dec = jnp.exp(lg * (idx + 1.0))
    k_dec = jnp.exp(lg * (blk - 1.0 - idx))
    blk_dec = jnp.exp(jnp.full((1, 1), blk, F32) * lg)
    gn = gn_ref[...]
    nt = (((1,), (1,)), ((), ()))
    tn = (((0,), (0,)), ((), ()))

    def body(n, state):
        sl = pl.ds(pl.multiple_of(n * blk, blk), blk)
        q = q_ref[sl, :]
        k = k_ref[sl, :]
        v = v_ref[sl, :]
        s = lax.dot_general(q, k, nt, preferred_element_type=F32) * d_ref[...]
        intra = jnp.dot(s.astype(BF16), v, preferred_element_type=F32)
        cross = jnp.dot(q, state.astype(BF16), preferred_element_type=F32) * q_dec
        kd = (k.astype(F32) * k_dec).astype(BF16)
        new_state = blk_dec * state + lax.dot_general(kd, v, tn, preferred_element_type=F32)
        ret = intra + cross
        mu = jnp.mean(ret, axis=-1, keepdims=True)
        dev = ret - mu
        var = jnp.mean(dev * dev, axis=-1, keepdims=True)
        y = dev * lax.rsqrt(var + EPS) * gn
        g = g_ref[sl, :].astype(F32)
        gate = g / (1.0 + jnp.exp(-g))
        o_ref[sl, :] = (gate * y).astype(o_ref.dtype)
        return new_state

    lax.fori_loop(0, q_ref.shape[0] // blk, body, jnp.zeros((RET_DK, RET_DV), F32), unroll=4)


def _retention(qkvg, gn_g, log_g, batch, seq):
    t = qkvg.shape[0]
    blk = 256
    h = RET_HEADS
    return pl.pallas_call(
        functools.partial(_retention_kernel, blk=blk),
        grid=(batch, h),
        in_specs=[pl.BlockSpec(memory_space=pltpu.SMEM),
                  pl.BlockSpec((seq, RET_DK), lambda b, i: (b, i)),
                  pl.BlockSpec((seq, RET_DK), lambda b, i: (b, h + i)),
                  pl.BlockSpec((seq, RET_DV), lambda b, i: (b, 2 * h + i)),
                  pl.BlockSpec((seq, RET_DV), lambda b, i: (b, 3 * h + i)),
                  pl.BlockSpec((1, RET_DV), lambda b, i: (0, i))],
        out_specs=pl.BlockSpec((seq, RET_DV), lambda b, i: (b, i)),
        out_shape=jax.ShapeDtypeStruct((t, RET_WIDTH), BF16),
        scratch_shapes=[pltpu.VMEM((blk, blk), F32)],
        compiler_params=_params("parallel", "arbitrary"),
        name="retention",
    )(log_g, qkvg, qkvg, qkvg, qkvg, gn_g)


def _rms(v, g):
    return v * lax.rsqrt(jnp.mean(v * v, axis=-1, keepdims=True) + EPS) * g


def _rope64(r, cm, sa, sb):
    half = MLA_ROPE // 2
    return r * cm + pltpu.roll(r, LANES - half, 1) * sa + pltpu.roll(r, half, 1) * sb


def _mla_prep_kernel(x_ref, wcq_ref, wckv_ref, wkr_ref, qg_ref, kvg_ref, wuq_ref, wuk_ref, wuv_ref,
                     cm_ref, sa_ref, sb_ref, q_ref, k_ref, v_ref):
    x = x_ref[...]
    cm = cm_ref[...]
    sa = sa_ref[...]
    sb = sb_ref[...]
    cq = jnp.dot(x, wcq_ref[...], preferred_element_type=F32)
    cqn = _rms(cq, qg_ref[...]).astype(BF16)
    q = jnp.dot(cqn, wuq_ref[...], preferred_element_type=F32) * MLA_Q_SCALE
    for h in range(MLA_HEADS):
        lo = h * MLA_QK_PAD
        q_ref[:, lo:lo + MLA_NOPE] = q[:, lo:lo + MLA_NOPE].astype(q_ref.dtype)
        q_ref[:, lo + MLA_NOPE:lo + MLA_QK_PAD] = _rope64(
            q[:, lo + MLA_NOPE:lo + MLA_QK_PAD], cm, sa, sb).astype(q_ref.dtype)
    ckv = jnp.dot(x, wckv_ref[...], preferred_element_type=F32)
    ckvn = _rms(ckv, kvg_ref[...]).astype(BF16)
    kr = jnp.dot(x, wkr_ref[...], preferred_element_type=F32)
    k_rope = _rope64(kr, cm, sa, sb).astype(k_ref.dtype)
    kn = jnp.dot(ckvn, wuk_ref[...], preferred_element_type=F32)
    for h in range(MLA_HEADS):
        lo = h * MLA_QK_PAD
        k_ref[:, lo:lo + MLA_NOPE] = kn[:, h * MLA_NOPE:(h + 1) * MLA_NOPE].astype(k_ref.dtype)
        k_ref[:, lo + MLA_NOPE:lo + MLA_QK_PAD] = k_rope
    v_ref[...] = jnp.dot(ckvn, wuv_ref[...], preferred_element_type=F32).astype(v_ref.dtype)


def _mla_prep(xb, wcq, wckv, wkr, qg, kvg, wuq, wuk, wuv, cm, sa, sb):
    t, d = xb.shape
    tm = 512
    row = lambda i: (i, 0)
    whole = lambda i: (0, 0)
    full = lambda a: pl.BlockSpec(a.shape, whole)
    qk_w = MLA_HEADS * MLA_QK_PAD
    return pl.pallas_call(
        _mla_prep_kernel,
        grid=(t // tm,),
        in_specs=[pl.BlockSpec((tm, d), row), full(wcq), full(wckv), full(wkr), full(qg), full(kvg),
                  full(wuq), full(wuk), full(wuv),
                  pl.BlockSpec((tm, LANES), row), pl.BlockSpec((tm, LANES), row),
                  pl.BlockSpec((tm, LANES), row)],
        out_specs=[pl.BlockSpec((tm, qk_w), row), pl.BlockSpec((tm, qk_w), row),
                   pl.BlockSpec((tm, MLA_WIDTH), row)],
        out_shape=[jax.ShapeDtypeStruct((t, qk_w), BF16), jax.ShapeDtypeStruct((t, qk_w), BF16),
                   jax.ShapeDtypeStruct((t, MLA_WIDTH), BF16)],
        compiler_params=_params("parallel"),
        name="mla_prep",
    )(xb, wcq, wckv, wkr, qg, kvg, wuq, wuk, wuv, cm, sa, sb)


def _mla_attn_kernel(q_ref, k_ref, v_ref, o_ref, sa_ref, sb_ref, pa_ref, pb_ref, acc_ref, qt_ref, *,
                     kblk):
    qi = pl.program_id(2)
    qblk = q_ref.shape[0]
    qt_ref[...] = q_ref[...].T
    nt = (((1,), (1,)), ((), ()))
    tn = (((0,), (0,)), ((), ()))

    def keys(j):
        return pl.ds(pl.multiple_of(j * kblk, kblk), kblk)

    def scores(s_ref, j):
        s_ref[...] = jnp.dot(k_ref[keys(j), :], qt_ref[...], preferred_element_type=F32)

    def probs(s_ref, p_ref, stats, key_off=None):
        m, l = stats
        st = s_ref[...]
        if key_off is not None:
            key = lax.broadcasted_iota(jnp.int32, st.shape, 0) + key_off
            qry = lax.broadcasted_iota(jnp.int32, st.shape, 1)
            st = jnp.where(key // CHUNK <= qry // CHUNK, st, -jnp.inf)
        m_new = jnp.maximum(m, jnp.max(st, axis=0, keepdims=True))
        alpha = jnp.exp2(m - m_new)
        p = jnp.exp2(st - m_new)
        p_ref[...] = p.astype(p_ref.dtype)
        return (m_new, alpha * l + jnp.sum(p, axis=0, keepdims=True)), alpha

    def accum(p_ref, j, alpha):
        pv = lax.dot_general(v_ref[keys(j), :], p_ref[...], tn, preferred_element_type=F32)
        acc_ref[...] = alpha * acc_ref[...] + pv

    def body(jj, carry):
        stats, alpha = carry
        j = 2 * jj
        scores(sb_ref, j + 1)
        accum(pb_ref, jnp.maximum(j - 1, 0), alpha)
        stats, alpha = probs(sa_ref, pa_ref, stats)
        scores(sa_ref, j + 2)
        accum(pa_ref, j, alpha)
        stats, alpha = probs(sb_ref, pb_ref, stats)
        return stats, alpha

    acc_ref[...] = jnp.zeros_like(acc_ref)
    pb_ref[...] = jnp.zeros_like(pb_ref)
    stats = (jnp.full((1, qblk), -jnp.inf, F32), jnp.zeros((1, qblk), F32))
    scores(sa_ref, 0)
    nfull = qi * (qblk // kblk)
    stats, alpha = lax.fori_loop(0, nfull // 2, body, (stats, jnp.ones((1, qblk), F32)))
    scores(sb_ref, nfull + 1)
    accum(pb_ref, jnp.maximum(nfull - 1, 0), alpha)
    stats, alpha = probs(sa_ref, pa_ref, stats, key_off=0)
    accum(pa_ref, nfull, alpha)
    (m, l), alpha = probs(sb_ref, pb_ref, stats, key_off=kblk)
    accum(pb_ref, nfull + 1, alpha)
    o_ref[...] = (acc_ref[...] / l).T.astype(o_ref.dtype)


def _mla_attn(q, k, v, batch, seq):
    t = q.shape[0]
    qblk = 512
    kblk = qblk // 2
    nq = seq // qblk
    return pl.pallas_call(
        functools.partial(_mla_attn_kernel, kblk=kblk),
        grid=(batch, MLA_HEADS, nq),
        in_specs=[pl.BlockSpec((qblk, MLA_QK_PAD), lambda b, h, i: (b * nq + i, h)),
                  pl.BlockSpec((seq, MLA_QK_PAD), lambda b, h, i: (b, h)),
                  pl.BlockSpec((seq, MLA_DV), lambda b, h, i: (b, h))],
        out_specs=pl.BlockSpec((qblk, MLA_DV), lambda b, h, i: (b * nq + i, h)),
        out_shape=jax.ShapeDtypeStruct((t, MLA_WIDTH), BF16),
        scratch_shapes=[pltpu.VMEM((kblk, qblk), F32), pltpu.VMEM((kblk, qblk), F32),
                        pltpu.VMEM((kblk, qblk), BF16), pltpu.VMEM((kblk, qblk), BF16),
                        pltpu.VMEM((MLA_DV, qblk), F32), pltpu.VMEM((MLA_QK_PAD, qblk), BF16)],
        compiler_params=_params("parallel", "parallel", "arbitrary"),
        name="mla_attn",
    )(q, k, v)


def _layer_norm(y, g, b):
    mu = jnp.mean(y, axis=-1, keepdims=True)
    dev = y - mu
    var = jnp.mean(dev * dev, axis=-1, keepdims=True)
    return dev * lax.rsqrt(var + EPS) * g + b


def _out_ln_kernel(ret_ref, mla_ref, x_ref, w_ref, g_ref, b_ref, o_ref, *, alpha):
    kr = ret_ref.shape[1]
    rows = x_ref.shape[0] // OUT_LN_SPLIT
    for r in range(OUT_LN_SPLIT):
        sl = slice(r * rows, (r + 1) * rows)
        mix = jnp.dot(ret_ref[sl, :], w_ref[0:kr, :], preferred_element_type=F32)
        mix = mix + jnp.dot(mla_ref[sl, :], w_ref[kr:, :], preferred_element_type=F32)
        y = alpha * x_ref[sl, :] + mix
        o_ref[sl, :] = _layer_norm(y, g_ref[...], b_ref[...])


def _out_ln(ret, mla, xf, w, g, b, alpha):
    t, d = xf.shape
    tm = 512
    row = lambda i: (i, 0)
    whole = lambda i: (0, 0)
    return pl.pallas_call(
        functools.partial(_out_ln_kernel, alpha=alpha),
        grid=(t // tm,),
        in_specs=[pl.BlockSpec((tm, ret.shape[1]), row), pl.BlockSpec((tm, mla.shape[1]), row),
                  pl.BlockSpec((tm, d), row), pl.BlockSpec(w.shape, whole),
                  pl.BlockSpec((1, d), whole), pl.BlockSpec((1, d), whole)],
        out_specs=pl.BlockSpec((tm, d), row),
        out_shape=jax.ShapeDtypeStruct((t, d), F32),
        compiler_params=_params("parallel"),
        name="out_ln",
    )(ret, mla, xf, w, g, b)


def _ffn_ln_kernel(x_ref, wu_ref, wd_ref, g_ref, b_ref, o_ref, xb_ref, acc_ref, *, alpha):
    f = pl.program_id(1)

    @pl.when(f == 0)
    def _():
        xb_ref[...] = x_ref[...].astype(BF16)
        acc_ref[...] = jnp.zeros_like(acc_ref)

    u = jnp.dot(xb_ref[...], wu_ref[...], preferred_element_type=F32)
    a = jnp.square(jnp.maximum(u, 0.0)).astype(BF16)
    acc_ref[...] += jnp.dot(a, wd_ref[...], preferred_element_type=F32)

    @pl.when(f == pl.num_programs(1) - 1)
    def _():
        y = alpha * x_ref[...] + acc_ref[...]
        o_ref[...] = _layer_norm(y, g_ref[...], b_ref[...])


def _ffn_ln(x1, wu, wd, g, b, alpha):
    t, d = x1.shape
    dff = wu.shape[1]
    tm, tf = 512, 1024
    return pl.pallas_call(
        functools.partial(_ffn_ln_kernel, alpha=alpha),
        grid=(t // tm, dff // tf),
        in_specs=[pl.BlockSpec((tm, d), lambda i, f: (i, 0)),
                  pl.BlockSpec((d, tf), lambda i, f: (0, f)),
                  pl.BlockSpec((tf, d), lambda i, f: (f, 0)),
                  pl.BlockSpec((1, d), lambda i, f: (0, 0)),
                  pl.BlockSpec((1, d), lambda i, f: (0, 0))],
        out_specs=pl.BlockSpec((tm, d), lambda i, f: (i, 0)),
        out_shape=jax.ShapeDtypeStruct((t, d), F32),
        scratch_shapes=[pltpu.VMEM((tm, d), BF16), pltpu.VMEM((tm, d), F32)],
        compiler_params=_params("parallel", "arbitrary"),
        name="ffn_ln",
    )(x1, wu, wd, g, b)


def kernel(x, positions, w_in, q_norm_g, w_uq, kv_norm_g, w_uk, w_uv, ret_gn_g, w_out, ln1_g, ln1_b,
           w_up, w_down, ln2_g, ln2_b):
    batch, seq, d = x.shape
    depth = w_in.shape[0]
    t = batch * seq
    alpha = (2.0 * depth) ** 0.25
    xf = x.reshape(t, d)
    c_ret, s_ret, c_mla, sa_mla, sb_mla = _rope_tables(positions.reshape(t, 1))
    log_g = jnp.log1p(-jnp.exp2(-5.0 - jnp.arange(RET_HEADS, dtype=F32)))
    o_cq = 2 * RET_QK + 2 * RET_WIDTH
    o_ckv = o_cq + MLA_Q_RANK
    o_kr = o_ckv + MLA_KV_RANK
    for l in range(depth):
        w = w_in[l]
        w_qkvg = w[:, :o_cq].astype(BF16)
        w_cq = w[:, o_cq:o_ckv].astype(BF16)
        w_ckv = w[:, o_ckv:o_kr].astype(BF16)
        w_kr = jnp.pad(w[:, o_kr:], ((0, 0), (0, LANES - MLA_ROPE))).astype(BF16)
        wuq = w_uq[l].reshape(MLA_Q_RANK, MLA_HEADS, MLA_NOPE + MLA_ROPE)
        wuq = jnp.pad(wuq, ((0, 0), (0, 0), (0, MLA_QK_PAD - MLA_NOPE - MLA_ROPE)))
        wuq = wuq.reshape(MLA_Q_RANK, MLA_HEADS * MLA_QK_PAD).astype(BF16)

        qkvg, xb = _proj(xf, w_qkvg, c_ret, s_ret)
        ret = _retention(qkvg, ret_gn_g[l].reshape(1, RET_WIDTH), log_g, batch, seq)
        q, k, v = _mla_prep(xb, w_cq, w_ckv, w_kr, q_norm_g[l].reshape(1, -1),
                            kv_norm_g[l].reshape(1, -1), wuq, w_uk[l].astype(BF16),
                            w_uv[l].astype(BF16), c_mla, sa_mla, sb_mla)
        mla = _mla_attn(q, k, v, batch, seq)
        x1 = _out_ln(ret, mla, xf, w_out[l].astype(BF16), ln1_g[l].reshape(1, d),
                     ln1_b[l].reshape(1, d), alpha)
        xf = _ffn_ln(x1, w_up[l].astype(BF16), w_down[l].astype(BF16), ln2_g[l].reshape(1, d),
                     ln2_b[l].reshape(1, d), alpha)
    return xf.reshape(batch, seq, d)
```

```python
import functools

import jax
import jax.numpy as jnp
from jax import lax
from jax.experimental import pallas as pl
from jax.experimental.pallas import tpu as pltpu

F32 = jnp.float32
BF16 = jnp.bfloat16

LANES = 128
CHUNK = 64
RET_HEADS = 8
RET_DK = 128
RET_DV = 128
RET_QK = RET_HEADS * RET_DK
RET_WIDTH = RET_HEADS * RET_DV
MLA_HEADS = 8
MLA_NOPE = 128
MLA_ROPE = 64
MLA_DV = 128
MLA_Q_RANK = 768
MLA_KV_RANK = 512
MLA_QK_PAD = 256
MLA_WIDTH = MLA_HEADS * MLA_DV
MLA_Q_SCALE = (MLA_NOPE + MLA_ROPE) ** -0.5 * 1.4426950408889634
ROPE_BASE = 10000.0
EPS = 1e-5

VMEM_LIMIT = 56 * 1024 * 1024
OUT_LN_SPLIT = 2


def _params(*sem):
    return pltpu.CompilerParams(dimension_semantics=sem, vmem_limit_bytes=VMEM_LIMIT)


def _rope_table_kernel(pos_ref, cst_ref, cr_ref, sr_ref, cm_ref, sa_ref, sb_ref):
    pos = pos_ref[...].astype(F32)
    ang = pos * cst_ref[0:1, :]
    c = jnp.cos(ang)
    s = jnp.sin(ang)
    half = LANES // 2
    quarter = LANES // 4
    c64 = pltpu.roll(c, half, 1)
    s64 = pltpu.roll(s, half, 1)
    cr_ref[...] = c * cst_ref[1:2, :] + c64 * cst_ref[3:4, :]
    sr_ref[...] = s * cst_ref[2:3, :] + s64 * cst_ref[3:4, :]
    cm_ref[...] = c64 * cst_ref[4:5, :] + pltpu.roll(c, half + quarter, 1) * cst_ref[6:7, :]
    sa_ref[...] = s64 * cst_ref[5:6, :]
    sb_ref[...] = pltpu.roll(s, half + quarter, 1) * cst_ref[6:7, :]


def _rope_tables(pos):
    t = pos.shape[0]
    tm = 2048
    half_r = RET_DK // 2
    half_m = MLA_ROPE // 2
    inv_r = ROPE_BASE ** (-jnp.arange(0, RET_DK, 2, dtype=F32) / RET_DK)
    inv_m = ROPE_BASE ** (-jnp.arange(0, MLA_ROPE, 2, dtype=F32) / MLA_ROPE)
    assert half_r == LANES // 2 and half_m == LANES // 4
    lane = jnp.arange(LANES)
    lo64 = (lane < half_r).astype(F32)
    q0 = (lane < half_m).astype(F32)
    q1 = ((lane >= half_m) & (lane < 2 * half_m)).astype(F32)
    cst = jnp.stack([
        jnp.concatenate([inv_r, inv_m, jnp.zeros((LANES - half_r - half_m,), F32)]),
        lo64, -lo64, 1.0 - lo64, q0, -q0, q1, jnp.zeros((LANES,), F32)])
    out = jax.ShapeDtypeStruct((t, LANES), F32)
    tab_spec = pl.BlockSpec((tm, LANES), lambda i: (i, 0))
    return pl.pallas_call(
        _rope_table_kernel,
        grid=(t // tm,),
        in_specs=[pl.BlockSpec((tm, 1), lambda i: (i, 0)),
                  pl.BlockSpec((8, LANES), lambda i: (0, 0))],
        out_specs=[tab_spec] * 5,
        out_shape=[out] * 5,
        compiler_params=_params("parallel"),
        name="rope_tables",
    )(pos, cst)


def _proj_kernel(x_ref, w_ref, c_ref, s_ref, o_ref, xb_ref, *, k_scale):
    j = pl.program_id(1)

    @pl.when(j == 0)
    def _():
        xb_ref[...] = x_ref[...].astype(xb_ref.dtype)

    res = jnp.dot(xb_ref[...], w_ref[...], preferred_element_type=F32)
    rotary = j < 2
    scale = jnp.where(j == 1, k_scale, 1.0).astype(F32)
    c = jnp.where(rotary, c_ref[...] * scale, 1.0)
    s = jnp.where(rotary, s_ref[...] * scale, 0.0)
    for h in range(res.shape[1] // RET_DK):
        r = res[:, h * RET_DK:(h + 1) * RET_DK]
        o = r * c + pltpu.roll(r, RET_DK // 2, 1) * s
        o_ref[:, h * RET_DK:(h + 1) * RET_DK] = o.astype(o_ref.dtype)


def _proj(xf, w, c_tab, s_tab):
    t, d = xf.shape
    tm, tn = 1024, RET_QK
    n = 4 * tn
    assert RET_QK == RET_WIDTH and w.shape[1] >= n
    return pl.pallas_call(
        functools.partial(_proj_kernel, k_scale=RET_DK ** -0.5),
        grid=(t // tm, n // tn),
        in_specs=[pl.BlockSpec((tm, d), lambda i, j: (i, 0)),
                  pl.BlockSpec((d, tn), lambda i, j: (0, j)),
                  pl.BlockSpec((tm, LANES), lambda i, j: (i, 0)),
                  pl.BlockSpec((tm, LANES), lambda i, j: (i, 0))],
        out_specs=[pl.BlockSpec((tm, tn), lambda i, j: (i, j)),
                   pl.BlockSpec((tm, d), lambda i, j: (i, 0))],
        out_shape=[jax.ShapeDtypeStruct((t, n), BF16), jax.ShapeDtypeStruct((t, d), BF16)],
        compiler_params=_params("parallel", "arbitrary"),
        name="proj",
    )(xf, w, c_tab, s_tab)


def _retention_kernel(lg_ref, q_ref, k_ref, v_ref, g_ref, gn_ref, o_ref, d_ref, *, blk):
    lg = lg_ref[pl.program_id(1)]
    row = lax.broadcasted_iota(jnp.int32, (blk, blk), 0)
    col = lax.broadcasted_iota(jnp.int32, (blk, blk), 1)
    dist = jnp.abs(row - col).astype(F32)
    d_ref[...] = jnp.where(col // CHUNK <= row // CHUNK, jnp.exp(lg * dist), 0.0)
    idx = lax.broadcasted_iota(jnp.int32, (blk, 1), 0).astype(F32)
    q_dec = jnp.exp(lg * (idx + 1.0))
    k_dec = jnp.exp(lg * (blk - 1.0 - idx))
    blk_dec = jnp.exp(jnp.full((1, 1), blk, F32) * lg)
    gn = gn_ref[...]
    nt = (((1,), (1,)), ((), ()))
    tn = (((0,), (0,)), ((), ()))

    def body(n, state):
        sl = pl.ds(pl.multiple_of(n * blk, blk), blk)
        q = q_ref[sl, :]
        k = k_ref[sl, :]
        v = v_ref[sl, :]
        s = lax.dot_general(q, k, nt, preferred_element_type=F32) * d_ref[...]
        intra = jnp.dot(s.astype(BF16), v, preferred_element_type=F32)
        cross = jnp.dot(q, state.astype(BF16), preferred_element_type=F32) * q_dec
        kd = (k.astype(F32) * k_dec).astype(BF16)
        new_state = blk_dec * state + lax.dot_general(kd, v, tn, preferred_element_type=F32)
        ret = intra + cross
        mu = jnp.mean(ret, axis=-1, keepdims=True)
        dev = ret - mu
        var = jnp.mean(dev * dev, axis=-1, keepdims=True)
        y = dev * lax.rsqrt(var + EPS) * gn
        g = g_ref[sl, :].astype(F32)
        gate = g / (1.0 + jnp.exp(-g))
        o_ref[sl, :] = (gate * y).astype(o_ref.dtype)
        return new_state

    lax.fori_loop(0, q_ref.shape[0] // blk, body, jnp.zeros((RET_DK, RET_DV), F32), unroll=4)


def _retention(qkvg, gn_g, log_g, batch, seq):
    t = qkvg.shape[0]
    blk = 256
    h = RET_HEADS
    return pl.pallas_call(
        functools.partial(_retention_kernel, blk=blk),
        grid=(batch, h),
        in_specs=[pl.BlockSpec(memory_space=pltpu.SMEM),
                  pl.BlockSpec((seq, RET_DK), lambda b, i: (b, i)),
                  pl.BlockSpec((seq, RET_DK), lambda b, i: (b, h + i)),
                  pl.BlockSpec((seq, RET_DV), lambda b, i: (b, 2 * h + i)),
                  pl.BlockSpec((seq, RET_DV), lambda b, i: (b, 3 * h + i)),
                  pl.BlockSpec((1, RET_DV), lambda b, i: (0, i))],
        out_specs=pl.BlockSpec((seq, RET_DV), lambda b, i: (b, i)),
        out_shape=jax.ShapeDtypeStruct((t, RET_WIDTH), BF16),
        scratch_shapes=[pltpu.VMEM((blk, blk), F32)],
        compiler_params=_params("parallel", "arbitrary"),
        name="retention",
    )(log_g, qkvg, qkvg, qkvg, qkvg, gn_g)


def _rms(v, g):
    return v * lax.rsqrt(jnp.mean(v * v, axis=-1, keepdims=True) + EPS) * g


def _rope64(r, cm, sa, sb):
    half = MLA_ROPE // 2
    return r * cm + pltpu.roll(r, LANES - half, 1) * sa + pltpu.roll(r, half, 1) * sb


def _mla_prep_kernel(x_ref, wcq_ref, wckv_ref, wkr_ref, qg_ref, kvg_ref, wuq_ref, wuk_ref, wuv_ref,
                     cm_ref, sa_ref, sb_ref, q_ref, k_ref, v_ref):
    x = x_ref[...]
    cm = cm_ref[...]
    sa = sa_ref[...]
    sb = sb_ref[...]
    cq = jnp.dot(x, wcq_ref[...], preferred_element_type=F32)
    cqn = _rms(cq, qg_ref[...]).astype(BF16)
    q = jnp.dot(cqn, wuq_ref[...], preferred_element_type=F32) * MLA_Q_SCALE
    for h in range(MLA_HEADS):
        lo = h * MLA_QK_PAD
        q_ref[:, lo:lo + MLA_NOPE] = q[:, lo:lo + MLA_NOPE].astype(q_ref.dtype)
        q_ref[:, lo + MLA_NOPE:lo + MLA_QK_PAD] = _rope64(
            q[:, lo + MLA_NOPE:lo + MLA_QK_PAD], cm, sa, sb).astype(q_ref.dtype)
    ckv = jnp.dot(x, wckv_ref[...], preferred_element_type=F32)
    ckvn = _rms(ckv, kvg_ref[...]).astype(BF16)
    kr = jnp.dot(x, wkr_ref[...], preferred_element_type=F32)
    k_rope = _rope64(kr, cm, sa, sb).astype(k_ref.dtype)
    kn = jnp.dot(ckvn, wuk_ref[...], preferred_element_type=F32)
    for h in range(MLA_HEADS):
        lo = h * MLA_QK_PAD
        k_ref[:, lo:lo + MLA_NOPE] = kn[:, h * MLA_NOPE:(h + 1) * MLA_NOPE].astype(k_ref.dtype)
        k_ref[:, lo + MLA_NOPE:lo + MLA_QK_PAD] = k_rope
    v_ref[...] = jnp.dot(ckvn, wuv_ref[...], preferred_element_type=F32).astype(v_ref.dtype)


def _mla_prep(xb, wcq, wckv, wkr, qg, kvg, wuq, wuk, wuv, cm, sa, sb):
    t, d = xb.shape
    tm = 512
    row = lambda i: (i, 0)
    whole = lambda i: (0, 0)
    full = lambda a: pl.BlockSpec(a.shape, whole)
    qk_w = MLA_HEADS * MLA_QK_PAD
    return pl.pallas_call(
        _mla_prep_kernel,
        grid=(t // tm,),
        in_specs=[pl.BlockSpec((tm, d), row), full(wcq), full(wckv), full(wkr), full(qg), full(kvg),
                  full(wuq), full(wuk), full(wuv),
                  pl.BlockSpec((tm, LANES), row), pl.BlockSpec((tm, LANES), row),
                  pl.BlockSpec((tm, LANES), row)],
        out_specs=[pl.BlockSpec((tm, qk_w), row), pl.BlockSpec((tm, qk_w), row),
                   pl.BlockSpec((tm, MLA_WIDTH), row)],
        out_shape=[jax.ShapeDtypeStruct((t, qk_w), BF16), jax.ShapeDtypeStruct((t, qk_w), BF16),
                   jax.ShapeDtypeStruct((t, MLA_WIDTH), BF16)],
        compiler_params=_params("parallel"),
        name="mla_prep",
    )(xb, wcq, wckv, wkr, qg, kvg, wuq, wuk, wuv, cm, sa, sb)


def _mla_attn_kernel(q_ref, k_ref, v_ref, o_ref, sa_ref, sb_ref, pa_ref, pb_ref, acc_ref, qt_ref, *,
                     qblk, kblk):
    tn = (((0,), (0,)), ((), ()))

    def keys(j):
        return pl.ds(pl.multiple_of(j * kblk, kblk), kblk)

    def scores(s_ref, j):
        s_ref[...] = jnp.dot(k_ref[keys(j), :], qt_ref[...], preferred_element_type=F32)

    def probs(s_ref, p_ref, stats, key_off=None):
        m, l = stats
        st = s_ref[...]
        if key_off is not None:
            key = lax.broadcasted_iota(jnp.int32, st.shape, 0) + key_off
            qry = lax.broadcasted_iota(jnp.int32, st.shape, 1)
            st = jnp.where(key // CHUNK <= qry // CHUNK, st, -jnp.inf)
        m_new = jnp.maximum(m, jnp.max(st, axis=0, keepdims=True))
        alpha = jnp.exp2(m - m_new)
        p = jnp.exp2(st - m_new)
        p_ref[...] = p.astype(p_ref.dtype)
        return (m_new, alpha * l + jnp.sum(p, axis=0, keepdims=True)), alpha

    def accum(p_ref, j, alpha):
        pv = lax.dot_general(v_ref[keys(j), :], p_ref[...], tn, preferred_element_type=F32)
        acc_ref[...] = alpha * acc_ref[...] + pv

    def body(jj, carry):
        stats, alpha = carry
        j = 2 * jj
        scores(sb_ref, j + 1)
        accum(pb_ref, jnp.maximum(j - 1, 0), alpha)
        stats, alpha = probs(sa_ref, pa_ref, stats)
        scores(sa_ref, j + 2)
        accum(pa_ref, j, alpha)
        stats, alpha = probs(sb_ref, pb_ref, stats)
        return stats, alpha

    def query_block(qi, _):
        rows = pl.ds(pl.multiple_of(qi * qblk, qblk), qblk)
        qt_ref[...] = q_ref[rows, :].T
        acc_ref[...] = jnp.zeros_like(acc_ref)
        pb_ref[...] = jnp.zeros_like(pb_ref)
        stats = (jnp.full((1, qblk), -jnp.inf, F32), jnp.zeros((1, qblk), F32))
        scores(sa_ref, 0)
        nfull = qi * (qblk // kblk)
        stats, alpha = lax.fori_loop(0, nfull // 2, body, (stats, jnp.ones((1, qblk), F32)))
        scores(sb_ref, nfull + 1)
        accum(pb_ref, jnp.maximum(nfull - 1, 0), alpha)
        stats, alpha = probs(sa_ref, pa_ref, stats, key_off=0)
        accum(pa_ref, nfull, alpha)
        (m, l), alpha = probs(sb_ref, pb_ref, stats, key_off=kblk)
        accum(pb_ref, nfull + 1, alpha)
        o_ref[rows, :] = (acc_ref[...] / l).T.astype(o_ref.dtype)
        return 0

    lax.fori_loop(0, q_ref.shape[0] // qblk, query_block, 0)


def _mla_attn(q, k, v, batch, seq):
    t = q.shape[0]
    qblk = 512
    kblk = qblk // 2
    per_head = lambda b, h: (b, h)
    return pl.pallas_call(
        functools.partial(_mla_attn_kernel, qblk=qblk, kblk=kblk),
        grid=(batch, MLA_HEADS),
        in_specs=[pl.BlockSpec((seq, MLA_QK_PAD), per_head),
                  pl.BlockSpec((seq, MLA_QK_PAD), per_head),
                  pl.BlockSpec((seq, MLA_DV), per_head)],
        out_specs=pl.BlockSpec((seq, MLA_DV), per_head),
        out_shape=jax.ShapeDtypeStruct((t, MLA_WIDTH), BF16),
        scratch_shapes=[pltpu.VMEM((kblk, qblk), F32), pltpu.VMEM((kblk, qblk), F32),
                        pltpu.VMEM((kblk, qblk), BF16), pltpu.VMEM((kblk, qblk), BF16),
                        pltpu.VMEM((MLA_DV, qblk), F32), pltpu.VMEM((MLA_QK_PAD, qblk), BF16)],
        compiler_params=_params("parallel", "arbitrary"),
        name="mla_attn",
    )(q, k, v)


def _layer_norm(y, g, b):
    mu = jnp.mean(y, axis=-1, keepdims=True)
    dev = y - mu
    var = jnp.mean(dev * dev, axis=-1, keepdims=True)
    return dev * lax.rsqrt(var + EPS) * g + b


def _out_ln_kernel(ret_ref, mla_ref, x_ref, w_ref, g_ref, b_ref, o_ref, *, alpha):
    kr = ret_ref.shape[1]
    rows = x_ref.shape[0] // OUT_LN_SPLIT
    for r in range(OUT_LN_SPLIT):
        sl = slice(r * rows, (r + 1) * rows)
        mix = jnp.dot(ret_ref[sl, :], w_ref[0:kr, :], preferred_element_type=F32)
        mix = mix + jnp.dot(mla_ref[sl, :], w_ref[kr:, :], preferred_element_type=F32)
        y = alpha * x_ref[sl, :] + mix
        o_ref[sl, :] = _layer_norm(y, g_ref[...], b_ref[...])


def _out_ln(ret, mla, xf, w, g, b, alpha):
    t, d = xf.shape
    tm = 512
    row = lambda i: (i, 0)
    whole = lambda i: (0, 0)
    return pl.pallas_call(
        functools.partial(_out_ln_kernel, alpha=alpha),
        grid=(t // tm,),
        in_specs=[pl.BlockSpec((tm, ret.shape[1]), row), pl.BlockSpec((tm, mla.shape[1]), row),
                  pl.BlockSpec((tm, d), row), pl.BlockSpec(w.shape, whole),
                  pl.BlockSpec((1, d), whole), pl.BlockSpec((1, d), whole)],
        out_specs=pl.BlockSpec((tm, d), row),
        out_shape=jax.ShapeDtypeStruct((t, d), F32),
        compiler_params=_params("parallel"),
        name="out_ln",
    )(ret, mla, xf, w, g, b)


def _ffn_ln_kernel(x_ref, wu_ref, wd_ref, g_ref, b_ref, o_ref, xb_ref, acc_ref, *, alpha):
    f = pl.program_id(1)

    @pl.when(f == 0)
    def _():
        xb_ref[...] = x_ref[...].astype(BF16)
        acc_ref[...] = jnp.zeros_like(acc_ref)

    u = jnp.dot(xb_ref[...], wu_ref[...], preferred_element_type=F32)
    a = jnp.square(jnp.maximum(u, 0.0)).astype(BF16)
    acc_ref[...] += jnp.dot(a, wd_ref[...], preferred_element_type=F32)

    @pl.when(f == pl.num_programs(1) - 1)
    def _():
        y = alpha * x_ref[...] + acc_ref[...]
        o_ref[...] = _layer_norm(y, g_ref[...], b_ref[...])


def _ffn_ln(x1, wu, wd, g, b, alpha):
    t, d = x1.shape
    dff = wu.shape[1]
    tm, tf = 512, 1024
    return pl.pallas_call(
        functools.partial(_ffn_ln_kernel, alpha=alpha),
        grid=(t // tm, dff // tf),
        in_specs=[pl.BlockSpec((tm, d), lambda i, f: (i, 0)),
                  pl.BlockSpec((d, tf), lambda i, f: (0, f)),
                  pl.BlockSpec((tf, d), lambda i, f: (f, 0)),
                  pl.BlockSpec((1, d), lambda i, f: (0, 0)),
                  pl.BlockSpec((1, d), lambda i, f: (0, 0))],
        out_specs=pl.BlockSpec((tm, d), lambda i, f: (i, 0)),
        out_shape=jax.ShapeDtypeStruct((t, d), F32),
        scratch_shapes=[pltpu.VMEM((tm, d), BF16), pltpu.VMEM((tm, d), F32)],
        compiler_params=_params("parallel", "arbitrary"),
        name="ffn_ln",
    )(x1, wu, wd, g, b)


def kernel(x, positions, w_in, q_norm_g, w_uq, kv_norm_g, w_uk, w_uv, ret_gn_g, w_out, ln1_g, ln1_b,
           w_up, w_down, ln2_g, ln2_b):
    batch, seq, d = x.shape
    depth = w_in.shape[0]
    t = batch * seq
    alpha = (2.0 * depth) ** 0.25
    xf = x.reshape(t, d)
    c_ret, s_ret, c_mla, sa_mla, sb_mla = _rope_tables(positions.reshape(t, 1))
    log_g = jnp.log1p(-jnp.exp2(-5.0 - jnp.arange(RET_HEADS, dtype=F32)))
    o_cq = 2 * RET_QK + 2 * RET_WIDTH
    o_ckv = o_cq + MLA_Q_RANK
    o_kr = o_ckv + MLA_KV_RANK
    for l in range(depth):
        w = w_in[l].astype(BF16)
        w_cq = w[:, o_cq:o_ckv]
        w_ckv = w[:, o_ckv:o_kr]
        w_kr = jnp.pad(w[:, o_kr:], ((0, 0), (0, LANES - MLA_ROPE)))
        wuq = w_uq[l].reshape(MLA_Q_RANK, MLA_HEADS, MLA_NOPE + MLA_ROPE)
        wuq = jnp.pad(wuq, ((0, 0), (0, 0), (0, MLA_QK_PAD - MLA_NOPE - MLA_ROPE)))
        wuq = wuq.reshape(MLA_Q_RANK, MLA_HEADS * MLA_QK_PAD).astype(BF16)

        qkvg, xb = _proj(xf, w, c_ret, s_ret)
        ret = _retention(qkvg, ret_gn_g[l].reshape(1, RET_WIDTH), log_g, batch, seq)
        q, k, v = _mla_prep(xb, w_cq, w_ckv, w_kr, q_norm_g[l].reshape(1, -1),
                            kv_norm_g[l].reshape(1, -1), wuq, w_uk[l].astype(BF16),
                            w_uv[l].astype(BF16), c_mla, sa_mla, sb_mla)
        mla = _mla_attn(q, k, v, batch, seq)
        x1 = _out_ln(ret, mla, xf, w_out[l].astype(BF16), ln1_g[l].reshape(1, d),
                     ln1_b[l].reshape(1, d), alpha)
        xf = _ffn_ln(x1, w_up[l].astype(BF16), w_down[l].astype(BF16), ln2_g[l].reshape(1, d),
                     ln2_b[l].reshape(1, d), alpha)
    return xf.reshape(batch, seq, d)
```

```python
import functools

import jax
import jax.numpy as jnp
from jax import lax
from jax.experimental import pallas as pl
from jax.experimental.pallas import tpu as pltpu

F32 = jnp.float32
BF16 = jnp.bfloat16

LANES = 128
CHUNK = 64
RET_HEADS = 8
RET_DK = 128
RET_DV = 128
RET_QK = RET_HEADS * RET_DK
RET_WIDTH = RET_HEADS * RET_DV
MLA_HEADS = 8
MLA_NOPE = 128
MLA_ROPE = 64
MLA_DV = 128
MLA_Q_RANK = 768
MLA_KV_RANK = 512
MLA_QK_PAD = 256
MLA_WIDTH = MLA_HEADS * MLA_DV
MLA_Q_SCALE = (MLA_NOPE + MLA_ROPE) ** -0.5 * 1.4426950408889634
ROPE_BASE = 10000.0
EPS = 1e-5

VMEM_LIMIT = 56 * 1024 * 1024
OUT_LN_SPLIT = 2


def _params(*sem):
    return pltpu.CompilerParams(dimension_semantics=sem, vmem_limit_bytes=VMEM_LIMIT)


def _rope_table_kernel(pos_ref, cst_ref, cr_ref, sr_ref, cm_ref, sa_ref, sb_ref):
    pos = pos_ref[...].astype(F32)
    ang = pos * cst_ref[0:1, :]
    c = jnp.cos(ang)
    s = jnp.sin(ang)
    half = LANES // 2
    quarter = LANES // 4
    c64 = pltpu.roll(c, half, 1)
    s64 = pltpu.roll(s, half, 1)
    cr_ref[...] = c * cst_ref[1:2, :] + c64 * cst_ref[3:4, :]
    sr_ref[...] = s * cst_ref[2:3, :] + s64 * cst_ref[3:4, :]
    cm_ref[...] = c64 * cst_ref[4:5, :] + pltpu.roll(c, half + quarter, 1) * cst_ref[6:7, :]
    sa_ref[...] = s64 * cst_ref[5:6, :]
    sb_ref[...] = pltpu.roll(s, half + quarter, 1) * cst_ref[6:7, :]


def _rope_tables(pos):
    t = pos.shape[0]
    tm = 2048
    half_r = RET_DK // 2
    half_m = MLA_ROPE // 2
    inv_r = ROPE_BASE ** (-jnp.arange(0, RET_DK, 2, dtype=F32) / RET_DK)
    inv_m = ROPE_BASE ** (-jnp.arange(0, MLA_ROPE, 2, dtype=F32) / MLA_ROPE)
    assert half_r == LANES // 2 and half_m == LANES // 4
    lane = jnp.arange(LANES)
    lo64 = (lane < half_r).astype(F32)
    q0 = (lane < half_m).astype(F32)
    q1 = ((lane >= half_m) & (lane < 2 * half_m)).astype(F32)
    cst = jnp.stack([
        jnp.concatenate([inv_r, inv_m, jnp.zeros((LANES - half_r - half_m,), F32)]),
        lo64, -lo64, 1.0 - lo64, q0, -q0, q1, jnp.zeros((LANES,), F32)])
    out = jax.ShapeDtypeStruct((t, LANES), F32)
    tab_spec = pl.BlockSpec((tm, LANES), lambda i: (i, 0))
    return pl.pallas_call(
        _rope_table_kernel,
        grid=(t // tm,),
        in_specs=[pl.BlockSpec((tm, 1), lambda i: (i, 0)),
                  pl.BlockSpec((8, LANES), lambda i: (0, 0))],
        out_specs=[tab_spec] * 5,
        out_shape=[out] * 5,
        compiler_params=_params("parallel"),
        name="rope_tables",
    )(pos, cst)


def _proj_kernel(x_ref, w_ref, c_ref, s_ref, o_ref, xb_ref, *, k_scale):
    j = pl.program_id(1)

    @pl.when(j == 0)
    def _():
        xb_ref[...] = x_ref[...].astype(xb_ref.dtype)

    res = jnp.dot(xb_ref[...], w_ref[...], preferred_element_type=F32)
    rotary = j < 2
    scale = jnp.where(j == 1, k_scale, 1.0).astype(F32)
    c = jnp.where(rotary, c_ref[...] * scale, 1.0)
    s = jnp.where(rotary, s_ref[...] * scale, 0.0)
    for h in range(res.shape[1] // RET_DK):
        r = res[:, h * RET_DK:(h + 1) * RET_DK]
        o = r * c + pltpu.roll(r, RET_DK // 2, 1) * s
        o_ref[:, h * RET_DK:(h + 1) * RET_DK] = o.astype(o_ref.dtype)


def _proj(xf, w, c_tab, s_tab):
    t, d = xf.shape
    tm, tn = 1024, RET_QK
    n = 4 * tn
    assert RET_QK == RET_WIDTH and w.shape[1] >= n
    return pl.pallas_call(
        functools.partial(_proj_kernel, k_scale=RET_DK ** -0.5),
        grid=(t // tm, n // tn),
        in_specs=[pl.BlockSpec((tm, d), lambda i, j: (i, 0)),
                  pl.BlockSpec((d, tn), lambda i, j: (0, j)),
                  pl.BlockSpec((tm, LANES), lambda i, j: (i, 0)),
                  pl.BlockSpec((tm, LANES), lambda i, j: (i, 0))],
        out_specs=[pl.BlockSpec((tm, tn), lambda i, j: (i, j)),
                   pl.BlockSpec((tm, d), lambda i, j: (i, 0))],
        out_shape=[jax.ShapeDtypeStruct((t, n), BF16), jax.ShapeDtypeStruct((t, d), BF16)],
        compiler_params=_params("parallel", "arbitrary"),
        name="proj",
    )(xf, w, c_tab, s_tab)


def _retention_kernel(lg_ref, q_ref, k_ref, v_ref, g_ref, gn_ref, o_ref, d_ref, *, blk):
    lg = lg_ref[pl.program_id(1)]
    row = lax.broadcasted_iota(jnp.int32, (blk, blk), 0)
    col = lax.broadcasted_iota(jnp.int32, (blk, blk), 1)
    dist = jnp.abs(row - col).astype(F32)
    d_ref[...] = jnp.where(col // CHUNK <= row // CHUNK, jnp.exp(lg * dist), 0.0)
    idx = lax.broadcasted_iota(jnp.int32, (blk, 1), 0).astype(F32)
    q_dec = jnp.exp(lg * (idx + 1.0))
    k_dec = jnp.exp(lg * (blk - 1.0 - idx))
    blk_dec = jnp.exp(jnp.full((1, 1), blk, F32) * lg)
    gn = gn_ref[...]
    nt = (((1,), (1,)), ((), ()))
    tn = (((0,), (0,)), ((), ()))

    def body(n, state):
        sl = pl.ds(pl.multiple_of(n * blk, blk), blk)
        q = q_ref[sl, :]
        k = k_ref[sl, :]
        v = v_ref[sl, :]
        s = lax.dot_general(q, k, nt, preferred_element_type=F32) * d_ref[...]
        intra = jnp.dot(s.astype(BF16), v, preferred_element_type=F32)
        cross = jnp.dot(q, state.astype(BF16), preferred_element_type=F32) * q_dec
        kd = (k.astype(F32) * k_dec).astype(BF16)
        new_state = blk_dec * state + lax.dot_general(kd, v, tn, preferred_element_type=F32)
        ret = intra + cross
        mu = jnp.mean(ret, axis=-1, keepdims=True)
        dev = ret - mu
        var = jnp.mean(dev * dev, axis=-1, keepdims=True)
        y = dev * lax.rsqrt(var + EPS) * gn
        g = g_ref[sl, :].astype(F32)
        gate = g / (1.0 + jnp.exp(-g))
        o_ref[sl, :] = (gate * y).astype(o_ref.dtype)
        return new_state

    lax.fori_loop(0, q_ref.shape[0] // blk, body, jnp.zeros((RET_DK, RET_DV), F32), unroll=4)


def _retention(qkvg, gn_g, log_g, batch, seq):
    t = qkvg.shape[0]
    blk = 256
    h = RET_HEADS
    return pl.pallas_call(
        functools.partial(_retention_kernel, blk=blk),
        grid=(batch, h),
        in_specs=[pl.BlockSpec(memory_space=pltpu.SMEM),
                  pl.BlockSpec((seq, RET_DK), lambda b, i: (b, i)),
                  pl.BlockSpec((seq, RET_DK), lambda b, i: (b, h + i)),
                  pl.BlockSpec((seq, RET_DV), lambda b, i: (b, 2 * h + i)),
                  pl.BlockSpec((seq, RET_DV), lambda b, i: (b, 3 * h + i)),
                  pl.BlockSpec((1, RET_DV), lambda b, i: (0, i))],
        out_specs=pl.BlockSpec((seq, RET_DV), lambda b, i: (b, i)),
        out_shape=jax.ShapeDtypeStruct((t, RET_WIDTH), BF16),
        scratch_shapes=[pltpu.VMEM((blk, blk), F32)],
        compiler_params=_params("parallel", "arbitrary"),
        name="retention",
    )(log_g, qkvg, qkvg, qkvg, qkvg, gn_g)


def _rms(v, g):
    return v * lax.rsqrt(jnp.mean(v * v, axis=-1, keepdims=True) + EPS) * g


def _rope64(r, cm, sa, sb):
    half = MLA_ROPE // 2
    return r * cm + pltpu.roll(r, LANES - half, 1) * sa + pltpu.roll(r, half, 1) * sb


def _mla_prep_kernel(x_ref, wcq_ref, wckv_ref, wkr_ref, qg_ref, kvg_ref, wuq_ref, wuk_ref, wuv_ref,
                     cm_ref, sa_ref, sb_ref, q_ref, k_ref, v_ref):
    x = x_ref[...]
    cm = cm_ref[...]
    sa = sa_ref[...]
    sb = sb_ref[...]
    cq = jnp.dot(x, wcq_ref[...], preferred_element_type=F32)
    cqn = _rms(cq, qg_ref[...]).astype(BF16)
    q = jnp.dot(cqn, wuq_ref[...], preferred_element_type=F32) * MLA_Q_SCALE
    for h in range(MLA_HEADS):
        lo = h * MLA_QK_PAD
        q_ref[:, lo:lo + MLA_NOPE] = q[:, lo:lo + MLA_NOPE].astype(q_ref.dtype)
        q_ref[:, lo + MLA_NOPE:lo + MLA_QK_PAD] = _rope64(
            q[:, lo + MLA_NOPE:lo + MLA_QK_PAD], cm, sa, sb).astype(q_ref.dtype)
    ckv = jnp.dot(x, wckv_ref[...], preferred_element_type=F32)
    ckvn = _rms(ckv, kvg_ref[...]).astype(BF16)
    kr = jnp.dot(x, wkr_ref[...], preferred_element_type=F32)
    k_rope = _rope64(kr, cm, sa, sb).astype(k_ref.dtype)
    kn = jnp.dot(ckvn, wuk_ref[...], preferred_element_type=F32)
    for h in range(MLA_HEADS):
        lo = h * MLA_QK_PAD
        k_ref[:, lo:lo + MLA_NOPE] = kn[:, h * MLA_NOPE:(h + 1) * MLA_NOPE].astype(k_ref.dtype)
        k_ref[:, lo + MLA_NOPE:lo + MLA_QK_PAD] = k_rope
    v_ref[...] = jnp.dot(ckvn, wuv_ref[...], preferred_element_type=F32).astype(v_ref.dtype)


def _mla_prep(xb, wcq, wckv, wkr, qg, kvg, wuq, wuk, wuv, cm, sa, sb):
    t, d = xb.shape
    tm = 512
    row = lambda i: (i, 0)
    whole = lambda i: (0, 0)
    full = lambda a: pl.BlockSpec(a.shape, whole)
    qk_w = MLA_HEADS * MLA_QK_PAD
    return pl.pallas_call(
        _mla_prep_kernel,
        grid=(t // tm,),
        in_specs=[pl.BlockSpec((tm, d), row), full(wcq), full(wckv), full(wkr), full(qg), full(kvg),
                  full(wuq), full(wuk), full(wuv),
                  pl.BlockSpec((tm, LANES), row), pl.BlockSpec((tm, LANES), row),
                  pl.BlockSpec((tm, LANES), row)],
        out_specs=[pl.BlockSpec((tm, qk_w), row), pl.BlockSpec((tm, qk_w), row),
                   pl.BlockSpec((tm, MLA_WIDTH), row)],
        out_shape=[jax.ShapeDtypeStruct((t, qk_w), BF16), jax.ShapeDtypeStruct((t, qk_w), BF16),
                   jax.ShapeDtypeStruct((t, MLA_WIDTH), BF16)],
        compiler_params=_params("parallel"),
        name="mla_prep",
    )(xb, wcq, wckv, wkr, qg, kvg, wuq, wuk, wuv, cm, sa, sb)


def _mla_attn_kernel(q_ref, k_ref, v_ref, o_ref, sa_ref, sb_ref, pa_ref, pb_ref, acc_ref, qt_ref, *,
                     qblk, kblk):
    tn = (((0,), (0,)), ((), ()))

    def keys(j):
        return pl.ds(pl.multiple_of(j * kblk, kblk), kblk)

    def scores(s_ref, j):
        s_ref[...] = jnp.dot(k_ref[keys(j), :], qt_ref[...], preferred_element_type=F32)

    def probs(s_ref, p_ref, stats, key_off=None):
        m, l = stats
        st = s_ref[...]
        if key_off is not None:
            key = lax.broadcasted_iota(jnp.int32, st.shape, 0) + key_off
            qry = lax.broadcasted_iota(jnp.int32, st.shape, 1)
            st = jnp.where(key // CHUNK <= qry // CHUNK, st, -jnp.inf)
        m_new = jnp.maximum(m, jnp.max(st, axis=0, keepdims=True))
        alpha = jnp.exp2(m - m_new)
        p = jnp.exp2(st - m_new)
        p_ref[...] = p.astype(p_ref.dtype)
        return (m_new, alpha * l + jnp.sum(p, axis=0, keepdims=True)), alpha

    def accum(p_ref, j, alpha):
        pv = lax.dot_general(v_ref[keys(j), :], p_ref[...], tn, preferred_element_type=F32)
        acc_ref[...] = alpha * acc_ref[...] + pv

    def body(jj, carry):
        stats, alpha = carry
        j = 2 * jj
        scores(sb_ref, j + 1)
        accum(pb_ref, jnp.maximum(j - 1, 0), alpha)
        stats, alpha = probs(sa_ref, pa_ref, stats)
        scores(sa_ref, j + 2)
        accum(pa_ref, j, alpha)
        stats, alpha = probs(sb_ref, pb_ref, stats)
        return stats, alpha

    def query_block(qi, _):
        rows = pl.ds(pl.multiple_of(qi * qblk, qblk), qblk)
        qt_ref[...] = q_ref[rows, :].T
        acc_ref[...] = jnp.zeros_like(acc_ref)
        pb_ref[...] = jnp.zeros_like(pb_ref)
        stats = (jnp.full((1, qblk), -jnp.inf, F32), jnp.zeros((1, qblk), F32))
        scores(sa_ref, 0)
        nfull = qi * (qblk // kblk)
        stats, alpha = lax.fori_loop(0, nfull // 2, body, (stats, jnp.ones((1, qblk), F32)))
        scores(sb_ref, nfull + 1)
        accum(pb_ref, jnp.maximum(nfull - 1, 0), alpha)
        stats, alpha = probs(sa_ref, pa_ref, stats, key_off=0)
        accum(pa_ref, nfull, alpha)
        (m, l), alpha = probs(sb_ref, pb_ref, stats, key_off=kblk)
        accum(pb_ref, nfull + 1, alpha)
        o_ref[rows, :] = (acc_ref[...] / l).T.astype(o_ref.dtype)
        return 0

    lax.fori_loop(0, q_ref.shape[0] // qblk, query_block, 0)


def _mla_attn(q, k, v, batch, seq):
    t = q.shape[0]
    qblk = 1024
    kblk = qblk // 2
    per_head = lambda b, h: (b, h)
    return pl.pallas_call(
        functools.partial(_mla_attn_kernel, qblk=qblk, kblk=kblk),
        grid=(batch, MLA_HEADS),
        in_specs=[pl.BlockSpec((seq, MLA_QK_PAD), per_head),
                  pl.BlockSpec((seq, MLA_QK_PAD), per_head),
                  pl.BlockSpec((seq, MLA_DV), per_head)],
        out_specs=pl.BlockSpec((seq, MLA_DV), per_head),
        out_shape=jax.ShapeDtypeStruct((t, MLA_WIDTH), BF16),
        scratch_shapes=[pltpu.VMEM((kblk, qblk), F32), pltpu.VMEM((kblk, qblk), F32),
                        pltpu.VMEM((kblk, qblk), BF16), pltpu.VMEM((kblk, qblk), BF16),
                        pltpu.VMEM((MLA_DV, qblk), F32), pltpu.VMEM((MLA_QK_PAD, qblk), BF16)],
        compiler_params=_params("parallel", "arbitrary"),
        name="mla_attn",
    )(q, k, v)


def _layer_norm(y, g, b):
    mu = jnp.mean(y, axis=-1, keepdims=True)
    dev = y - mu
    var = jnp.mean(dev * dev, axis=-1, keepdims=True)
    return dev * lax.rsqrt(var + EPS) * g + b


def _out_ln_kernel(ret_ref, mla_ref, x_ref, w_ref, g_ref, b_ref, o_ref, *, alpha):
    kr = ret_ref.shape[1]
    rows = x_ref.shape[0] // OUT_LN_SPLIT
    for r in range(OUT_LN_SPLIT):
        sl = slice(r * rows, (r + 1) * rows)
        mix = jnp.dot(ret_ref[sl, :], w_ref[0:kr, :], preferred_element_type=F32)
        mix = mix + jnp.dot(mla_ref[sl, :], w_ref[kr:, :], preferred_element_type=F32)
        y = alpha * x_ref[sl, :] + mix
        o_ref[sl, :] = _layer_norm(y, g_ref[...], b_ref[...])


def _out_ln(ret, mla, xf, w, g, b, alpha):
    t, d = xf.shape
    tm = 512
    row = lambda i: (i, 0)
    whole = lambda i: (0, 0)
    return pl.pallas_call(
        functools.partial(_out_ln_kernel, alpha=alpha),
        grid=(t // tm,),
        in_specs=[pl.BlockSpec((tm, ret.shape[1]), row), pl.BlockSpec((tm, mla.shape[1]), row),
                  pl.BlockSpec((tm, d), row), pl.BlockSpec(w.shape, whole),
                  pl.BlockSpec((1, d), whole), pl.BlockSpec((1, d), whole)],
        out_specs=pl.BlockSpec((tm, d), row),
        out_shape=jax.ShapeDtypeStruct((t, d), F32),
        compiler_params=_params("parallel"),
        name="out_ln",
    )(ret, mla, xf, w, g, b)


def _ffn_ln_kernel(x_ref, wu_ref, wd_ref, g_ref, b_ref, o_ref, xb_ref, acc_ref, *, alpha):
    f = pl.program_id(1)

    @pl.when(f == 0)
    def _():
        xb_ref[...] = x_ref[...].astype(BF16)
        acc_ref[...] = jnp.zeros_like(acc_ref)

    u = jnp.dot(xb_ref[...], wu_ref[...], preferred_element_type=F32)
    a = jnp.square(jnp.maximum(u, 0.0)).astype(BF16)
    acc_ref[...] += jnp.dot(a, wd_ref[...], preferred_element_type=F32)

    @pl.when(f == pl.num_programs(1) - 1)
    def _():
        y = alpha * x_ref[...] + acc_ref[...]
        o_ref[...] = _layer_norm(y, g_ref[...], b_ref[...])


def _ffn_ln(x1, wu, wd, g, b, alpha):
    t, d = x1.shape
    dff = wu.shape[1]
    tm, tf = 512, 1024
    return pl.pallas_call(
        functools.partial(_ffn_ln_kernel, alpha=alpha),
        grid=(t // tm, dff // tf),
        in_specs=[pl.BlockSpec((tm, d), lambda i, f: (i, 0)),
                  pl.BlockSpec((d, tf), lambda i, f: (0, f)),
                  pl.BlockSpec((tf, d), lambda i, f: (f, 0)),
                  pl.BlockSpec((1, d), lambda i, f: (0, 0)),
                  pl.BlockSpec((1, d), lambda i, f: (0, 0))],
        out_specs=pl.BlockSpec((tm, d), lambda i, f: (i, 0)),
        out_shape=jax.ShapeDtypeStruct((t, d), F32),
        scratch_shapes=[pltpu.VMEM((tm, d), BF16), pltpu.VMEM((tm, d), F32)],
        compiler_params=_params("parallel", "arbitrary"),
        name="ffn_ln",
    )(x1, wu, wd, g, b)


def kernel(x, positions, w_in, q_norm_g, w_uq, kv_norm_g, w_uk, w_uv, ret_gn_g, w_out, ln1_g, ln1_b,
           w_up, w_down, ln2_g, ln2_b):
    batch, seq, d = x.shape
    depth = w_in.shape[0]
    t = batch * seq
    alpha = (2.0 * depth) ** 0.25
    xf = x.reshape(t, d)
    c_ret, s_ret, c_mla, sa_mla, sb_mla = _rope_tables(positions.reshape(t, 1))
    log_g = jnp.log1p(-jnp.exp2(-5.0 - jnp.arange(RET_HEADS, dtype=F32)))
    o_cq = 2 * RET_QK + 2 * RET_WIDTH
    o_ckv = o_cq + MLA_Q_RANK
    o_kr = o_ckv + MLA_KV_RANK
    for l in range(depth):
        w = w_in[l].astype(BF16)
        w_cq = w[:, o_cq:o_ckv]
        w_ckv = w[:, o_ckv:o_kr]
        w_kr = jnp.pad(w[:, o_kr:], ((0, 0), (0, LANES - MLA_ROPE)))
        wuq = w_uq[l].reshape(MLA_Q_RANK, MLA_HEADS, MLA_NOPE + MLA_ROPE)
        wuq = jnp.pad(wuq, ((0, 0), (0, 0), (0, MLA_QK_PAD - MLA_NOPE - MLA_ROPE)))
        wuq = wuq.reshape(MLA_Q_RANK, MLA_HEADS * MLA_QK_PAD).astype(BF16)

        qkvg, xb = _proj(xf, w, c_ret, s_ret)
        ret = _retention(qkvg, ret_gn_g[l].reshape(1, RET_WIDTH), log_g, batch, seq)
        q, k, v = _mla_prep(xb, w_cq, w_ckv, w_kr, q_norm_g[l].reshape(1, -1),
                            kv_norm_g[l].reshape(1, -1), wuq, w_uk[l].astype(BF16),
                            w_uv[l].astype(BF16), c_mla, sa_mla, sb_mla)
        mla = _mla_attn(q, k, v, batch, seq)
        x1 = _out_ln(ret, mla, xf, w_out[l].astype(BF16), ln1_g[l].reshape(1, d),
                     ln1_b[l].reshape(1, d), alpha)
        xf = _ffn_ln(x1, w_up[l].astype(BF16), w_down[l].astype(BF16), ln2_g[l].reshape(1, d),
                     ln2_b[l].reshape(1, d), alpha)
    return xf.reshape(batch, seq, d)
```

```python
import functools

import jax
import jax.numpy as jnp
from jax import lax
from jax.experimental import pallas as pl
from jax.experimental.pallas import tpu as pltpu

F32 = jnp.float32
BF16 = jnp.bfloat16

LANES = 128
CHUNK = 64
RET_HEADS = 8
RET_DK = 128
RET_DV = 128
RET_QK = RET_HEADS * RET_DK
RET_WIDTH = RET_HEADS * RET_DV
MLA_HEADS = 8
MLA_NOPE = 128
MLA_ROPE = 64
MLA_DV = 128
MLA_Q_RANK = 768
MLA_KV_RANK = 512
MLA_QK_PAD = 256
MLA_WIDTH = MLA_HEADS * MLA_DV
MLA_Q_SCALE = (MLA_NOPE + MLA_ROPE) ** -0.5 * 1.4426950408889634
ROPE_BASE = 10000.0
EPS = 1e-5

VMEM_LIMIT = 56 * 1024 * 1024
OUT_LN_SPLIT = 2
PROJ_SPLIT = 2


def _params(*sem):
    return pltpu.CompilerParams(dimension_semantics=sem, vmem_limit_bytes=VMEM_LIMIT)


def _rope_table_kernel(pos_ref, cst_ref, cr_ref, sr_ref, cm_ref, sa_ref, sb_ref):
    pos = pos_ref[...].astype(F32)
    ang = pos * cst_ref[0:1, :]
    c = jnp.cos(ang)
    s = jnp.sin(ang)
    half = LANES // 2
    quarter = LANES // 4
    c64 = pltpu.roll(c, half, 1)
    s64 = pltpu.roll(s, half, 1)
    cr_ref[...] = c * cst_ref[1:2, :] + c64 * cst_ref[3:4, :]
    sr_ref[...] = s * cst_ref[2:3, :] + s64 * cst_ref[3:4, :]
    cm_ref[...] = c64 * cst_ref[4:5, :] + pltpu.roll(c, half + quarter, 1) * cst_ref[6:7, :]
    sa_ref[...] = s64 * cst_ref[5:6, :]
    sb_ref[...] = pltpu.roll(s, half + quarter, 1) * cst_ref[6:7, :]


def _rope_tables(pos):
    t = pos.shape[0]
    tm = 2048
    half_r = RET_DK // 2
    half_m = MLA_ROPE // 2
    inv_r = ROPE_BASE ** (-jnp.arange(0, RET_DK, 2, dtype=F32) / RET_DK)
    inv_m = ROPE_BASE ** (-jnp.arange(0, MLA_ROPE, 2, dtype=F32) / MLA_ROPE)
    assert half_r == LANES // 2 and half_m == LANES // 4
    lane = jnp.arange(LANES)
    lo64 = (lane < half_r).astype(F32)
    q0 = (lane < half_m).astype(F32)
    q1 = ((lane >= half_m) & (lane < 2 * half_m)).astype(F32)
    cst = jnp.stack([
        jnp.concatenate([inv_r, inv_m, jnp.zeros((LANES - half_r - half_m,), F32)]),
        lo64, -lo64, 1.0 - lo64, q0, -q0, q1, jnp.zeros((LANES,), F32)])
    out = jax.ShapeDtypeStruct((t, LANES), F32)
    tab_spec = pl.BlockSpec((tm, LANES), lambda i: (i, 0))
    return pl.pallas_call(
        _rope_table_kernel,
        grid=(t // tm,),
        in_specs=[pl.BlockSpec((tm, 1), lambda i: (i, 0)),
                  pl.BlockSpec((8, LANES), lambda i: (0, 0))],
        out_specs=[tab_spec] * 5,
        out_shape=[out] * 5,
        compiler_params=_params("parallel"),
        name="rope_tables",
    )(pos, cst)


def _proj_kernel(x_ref, w_ref, c_ref, s_ref, o_ref, xb_ref, *, k_scale):
    j = pl.program_id(1)

    @pl.when(j == 0)
    def _():
        xb_ref[...] = x_ref[...].astype(xb_ref.dtype)

    rotary = j < 2
    scale = jnp.where(j == 1, k_scale, 1.0).astype(F32)
    rows = x_ref.shape[0] // PROJ_SPLIT
    for r0 in range(0, x_ref.shape[0], rows):
        sl = slice(r0, r0 + rows)
        res = jnp.dot(xb_ref[sl, :], w_ref[...], preferred_element_type=F32)
        c = jnp.where(rotary, c_ref[sl, :] * scale, 1.0)
        s = jnp.where(rotary, s_ref[sl, :] * scale, 0.0)
        for h in range(res.shape[1] // RET_DK):
            r = res[:, h * RET_DK:(h + 1) * RET_DK]
            o = r * c + pltpu.roll(r, RET_DK // 2, 1) * s
            o_ref[sl, h * RET_DK:(h + 1) * RET_DK] = o.astype(o_ref.dtype)


def _proj(xf, w, c_tab, s_tab):
    t, d = xf.shape
    tm, tn = 1024, RET_QK
    n = 4 * tn
    assert RET_QK == RET_WIDTH and w.shape[1] >= n
    return pl.pallas_call(
        functools.partial(_proj_kernel, k_scale=RET_DK ** -0.5),
        grid=(t // tm, n // tn),
        in_specs=[pl.BlockSpec((tm, d), lambda i, j: (i, 0)),
                  pl.BlockSpec((d, tn), lambda i, j: (0, j)),
                  pl.BlockSpec((tm, LANES), lambda i, j: (i, 0)),
                  pl.BlockSpec((tm, LANES), lambda i, j: (i, 0))],
        out_specs=[pl.BlockSpec((tm, tn), lambda i, j: (i, j)),
                   pl.BlockSpec((tm, d), lambda i, j: (i, 0))],
        out_shape=[jax.ShapeDtypeStruct((t, n), BF16), jax.ShapeDtypeStruct((t, d), BF16)],
        compiler_params=_params("parallel", "arbitrary"),
        name="proj",
    )(xf, w, c_tab, s_tab)


def _retention_kernel(lg_ref, q_ref, k_ref, v_ref, g_ref, gn_ref, o_ref, d_ref, *, blk):
    lg = lg_ref[pl.program_id(1)]
    row = lax.broadcasted_iota(jnp.int32, (blk, blk), 0)
    col = lax.broadcasted_iota(jnp.int32, (blk, blk), 1)
    dist = jnp.abs(row - col).astype(F32)
    d_ref[...] = jnp.where(col // CHUNK <= row // CHUNK, jnp.exp(lg * dist), 0.0)
    idx = lax.broadcasted_iota(jnp.int32, (blk, 1), 0).astype(F32)
    q_dec = jnp.exp(lg * (idx + 1.0))
    k_dec = jnp.exp(lg * (blk - 1.0 - idx))
    blk_dec = jnp.exp(jnp.full((1, 1), blk, F32) * lg)
    gn = gn_ref[...]
    nt = (((1,), (1,)), ((), ()))
    tn = (((0,), (0,)), ((), ()))

    def body(n, state):
        sl = pl.ds(pl.multiple_of(n * blk, blk), blk)
        q = q_ref[sl, :]
        k = k_ref[sl, :]
        v = v_ref[sl, :]
        s = lax.dot_general(q, k, nt, preferred_element_type=F32) * d_ref[...]
        intra = jnp.dot(s.astype(BF16), v, preferred_element_type=F32)
        cross = jnp.dot(q, state.astype(BF16), preferred_element_type=F32) * q_dec
        kd = (k.astype(F32) * k_dec).astype(BF16)
        new_state = blk_dec * state + lax.dot_general(kd, v, tn, preferred_element_type=F32)
        ret = intra + cross
        mu = jnp.mean(ret, axis=-1, keepdims=True)
        dev = ret - mu
        var = jnp.mean(dev * dev, axis=-1, keepdims=True)
        y = dev * lax.rsqrt(var + EPS) * gn
        g = g_ref[sl, :].astype(F32)
        gate = g / (1.0 + jnp.exp(-g))
        o_ref[sl, :] = (gate * y).astype(o_ref.dtype)
        return new_state

    lax.fori_loop(0, q_ref.shape[0] // blk, body, jnp.zeros((RET_DK, RET_DV), F32), unroll=4)


def _retention(qkvg, gn_g, log_g, batch, seq):
    t = qkvg.shape[0]
    blk = 256
    h = RET_HEADS
    return pl.pallas_call(
        functools.partial(_retention_kernel, blk=blk),
        grid=(batch, h),
        in_specs=[pl.BlockSpec(memory_space=pltpu.SMEM),
                  pl.BlockSpec((seq, RET_DK), lambda b, i: (b, i)),
                  pl.BlockSpec((seq, RET_DK), lambda b, i: (b, h + i)),
                  pl.BlockSpec((seq, RET_DV), lambda b, i: (b, 2 * h + i)),
                  pl.BlockSpec((seq, RET_DV), lambda b, i: (b, 3 * h + i)),
                  pl.BlockSpec((1, RET_DV), lambda b, i: (0, i))],
        out_specs=pl.BlockSpec((seq, RET_DV), lambda b, i: (b, i)),
        out_shape=jax.ShapeDtypeStruct((t, RET_WIDTH), BF16),
        scratch_shapes=[pltpu.VMEM((blk, blk), F32)],
        compiler_params=_params("parallel", "arbitrary"),
        name="retention",
    )(log_g, qkvg, qkvg, qkvg, qkvg, gn_g)


def _rms(v, g):
    return v * lax.rsqrt(jnp.mean(v * v, axis=-1, keepdims=True) + EPS) * g


def _rope64(r, cm, sa, sb):
    half = MLA_ROPE // 2
    return r * cm + pltpu.roll(r, LANES - half, 1) * sa + pltpu.roll(r, half, 1) * sb


def _mla_prep_kernel(x_ref, wcq_ref, wckv_ref, wkr_ref, qg_ref, kvg_ref, wuq_ref, wuk_ref, wuv_ref,
                     cm_ref, sa_ref, sb_ref, q_ref, k_ref, v_ref):
    x = x_ref[...]
    cm = cm_ref[...]
    sa = sa_ref[...]
    sb = sb_ref[...]
    cq = jnp.dot(x, wcq_ref[...], preferred_element_type=F32)
    cqn = _rms(cq, qg_ref[...]).astype(BF16)
    q = jnp.dot(cqn, wuq_ref[...], preferred_element_type=F32) * MLA_Q_SCALE
    for h in range(MLA_HEADS):
        lo = h * MLA_QK_PAD
        q_ref[:, lo:lo + MLA_NOPE] = q[:, lo:lo + MLA_NOPE].astype(q_ref.dtype)
        q_ref[:, lo + MLA_NOPE:lo + MLA_QK_PAD] = _rope64(
            q[:, lo + MLA_NOPE:lo + MLA_QK_PAD], cm, sa, sb).astype(q_ref.dtype)
    ckv = jnp.dot(x, wckv_ref[...], preferred_element_type=F32)
    ckvn = _rms(ckv, kvg_ref[...]).astype(BF16)
    kr = jnp.dot(x, wkr_ref[...], preferred_element_type=F32)
    k_rope = _rope64(kr, cm, sa, sb).astype(k_ref.dtype)
    kn = jnp.dot(ckvn, wuk_ref[...], preferred_element_type=F32)
    for h in range(MLA_HEADS):
        lo = h * MLA_QK_PAD
        k_ref[:, lo:lo + MLA_NOPE] = kn[:, h * MLA_NOPE:(h + 1) * MLA_NOPE].astype(k_ref.dtype)
        k_ref[:, lo + MLA_NOPE:lo + MLA_QK_PAD] = k_rope
    v_ref[...] = jnp.dot(ckvn, wuv_ref[...], preferred_element_type=F32).astype(v_ref.dtype)


def _mla_prep(xb, wcq, wckv, wkr, qg, kvg, wuq, wuk, wuv, cm, sa, sb):
    t, d = xb.shape
    tm = 1024
    row = lambda i: (i, 0)
    whole = lambda i: (0, 0)
    full = lambda a: pl.BlockSpec(a.shape, whole, pipeline_mode=pl.Buffered(1))
    qk_w = MLA_HEADS * MLA_QK_PAD
    return pl.pallas_call(
        _mla_prep_kernel,
        grid=(t // tm,),
        in_specs=[pl.BlockSpec((tm, d), row), full(wcq), full(wckv), full(wkr), full(qg), full(kvg),
                  full(wuq), full(wuk), full(wuv),
                  pl.BlockSpec((tm, LANES), row), pl.BlockSpec((tm, LANES), row),
                  pl.BlockSpec((tm, LANES), row)],
        out_specs=[pl.BlockSpec((tm, qk_w), row), pl.BlockSpec((tm, qk_w), row),
                   pl.BlockSpec((tm, MLA_WIDTH), row)],
        out_shape=[jax.ShapeDtypeStruct((t, qk_w), BF16), jax.ShapeDtypeStruct((t, qk_w), BF16),
                   jax.ShapeDtypeStruct((t, MLA_WIDTH), BF16)],
        compiler_params=_params("parallel"),
        name="mla_prep",
    )(xb, wcq, wckv, wkr, qg, kvg, wuq, wuk, wuv, cm, sa, sb)


def _mla_attn_kernel(q_ref, k_ref, v_ref, o_ref, sa_ref, sb_ref, pa_ref, pb_ref, acc_ref, qt_ref, *,
                     qblk, kblk):
    tn = (((0,), (0,)), ((), ()))

    def keys(j):
        return pl.ds(pl.multiple_of(j * kblk, kblk), kblk)

    def scores(s_ref, j):
        s_ref[...] = jnp.dot(k_ref[keys(j), :], qt_ref[...], preferred_element_type=F32)

    def probs(s_ref, p_ref, stats, key_off=None):
        m, l = stats
        st = s_ref[...]
        if key_off is not None:
            key = lax.broadcasted_iota(jnp.int32, st.shape, 0) + key_off
            qry = lax.broadcasted_iota(jnp.int32, st.shape, 1)
            st = jnp.where(key // CHUNK <= qry // CHUNK, st, -jnp.inf)
        m_new = jnp.maximum(m, jnp.max(st, axis=0, keepdims=True))
        alpha = jnp.exp2(m - m_new)
        p = jnp.exp2(st - m_new)
        p_ref[...] = p.astype(p_ref.dtype)
        return (m_new, alpha * l + jnp.sum(p, axis=0, keepdims=True)), alpha

    def accum(p_ref, j, alpha):
        pv = lax.dot_general(v_ref[keys(j), :], p_ref[...], tn, preferred_element_type=F32)
        acc_ref[...] = alpha * acc_ref[...] + pv

    def body(jj, carry):
        stats, alpha = carry
        j = 2 * jj
        scores(sb_ref, j + 1)
        accum(pb_ref, jnp.maximum(j - 1, 0), alpha)
        stats, alpha = probs(sa_ref, pa_ref, stats)
        scores(sa_ref, j + 2)
        accum(pa_ref, j, alpha)
        stats, alpha = probs(sb_ref, pb_ref, stats)
        return stats, alpha

    nq = q_ref.shape[0] // qblk

    def query_rows(qi):
        return pl.ds(pl.multiple_of(qi * qblk, qblk), qblk)

    def start_query_block(qi):
        qt_ref[...] = q_ref[query_rows(qi), :].T
        scores(sa_ref, 0)

    def query_block(qi, _):
        acc_ref[...] = jnp.zeros_like(acc_ref)
        pb_ref[...] = jnp.zeros_like(pb_ref)
        stats = (jnp.full((1, qblk), -jnp.inf, F32), jnp.zeros((1, qblk), F32))
        nfull = qi * (qblk // kblk)
        stats, alpha = lax.fori_loop(0, nfull // 2, body, (stats, jnp.ones((1, qblk), F32)))
        scores(sb_ref, nfull + 1)
        accum(pb_ref, jnp.maximum(nfull - 1, 0), alpha)
        stats, alpha = probs(sa_ref, pa_ref, stats, key_off=0)
        start_query_block(jnp.minimum(qi + 1, nq - 1))
        accum(pa_ref, nfull, alpha)
        (m, l), alpha = probs(sb_ref, pb_ref, stats, key_off=kblk)
        accum(pb_ref, nfull + 1, alpha)
        o_ref[query_rows(qi), :] = (acc_ref[...] / l).T.astype(o_ref.dtype)
        return 0

    start_query_block(0)
    lax.fori_loop(0, nq, query_block, 0)


def _mla_attn(q, k, v, batch, seq):
    t = q.shape[0]
    qblk = 1024
    kblk = qblk // 2
    per_head = lambda b, h: (b, h)
    return pl.pallas_call(
        functools.partial(_mla_attn_kernel, qblk=qblk, kblk=kblk),
        grid=(batch, MLA_HEADS),
        in_specs=[pl.BlockSpec((seq, MLA_QK_PAD), per_head),
                  pl.BlockSpec((seq, MLA_QK_PAD), per_head),
                  pl.BlockSpec((seq, MLA_DV), per_head)],
        out_specs=pl.BlockSpec((seq, MLA_DV), per_head),
        out_shape=jax.ShapeDtypeStruct((t, MLA_WIDTH), BF16),
        scratch_shapes=[pltpu.VMEM((kblk, qblk), F32), pltpu.VMEM((kblk, qblk), F32),
                        pltpu.VMEM((kblk, qblk), BF16), pltpu.VMEM((kblk, qblk), BF16),
                        pltpu.VMEM((MLA_DV, qblk), F32), pltpu.VMEM((MLA_QK_PAD, qblk), BF16)],
        compiler_params=_params("parallel", "arbitrary"),
        name="mla_attn",
    )(q, k, v)


def _layer_norm(y, g, b):
    mu = jnp.mean(y, axis=-1, keepdims=True)
    dev = y - mu
    var = jnp.mean(dev * dev, axis=-1, keepdims=True)
    return dev * lax.rsqrt(var + EPS) * g + b


def _out_ln_kernel(ret_ref, mla_ref, x_ref, w_ref, g_ref, b_ref, o_ref, *, alpha):
    kr = ret_ref.shape[1]
    rows = x_ref.shape[0] // OUT_LN_SPLIT
    for r in range(OUT_LN_SPLIT):
        sl = slice(r * rows, (r + 1) * rows)
        mix = jnp.dot(ret_ref[sl, :], w_ref[0:kr, :], preferred_element_type=F32)
        mix = mix + jnp.dot(mla_ref[sl, :], w_ref[kr:, :], preferred_element_type=F32)
        y = alpha * x_ref[sl, :] + mix
        o_ref[sl, :] = _layer_norm(y, g_ref[...], b_ref[...])


def _out_ln(ret, mla, xf, w, g, b, alpha):
    t, d = xf.shape
    tm = 512
    row = lambda i: (i, 0)
    whole = lambda i: (0, 0)
    return pl.pallas_call(
        functools.partial(_out_ln_kernel, alpha=alpha),
        grid=(t // tm,),
        in_specs=[pl.BlockSpec((tm, ret.shape[1]), row), pl.BlockSpec((tm, mla.shape[1]), row),
                  pl.BlockSpec((tm, d), row), pl.BlockSpec(w.shape, whole),
                  pl.BlockSpec((1, d), whole), pl.BlockSpec((1, d), whole)],
        out_specs=pl.BlockSpec((tm, d), row),
        out_shape=jax.ShapeDtypeStruct((t, d), F32),
        compiler_params=_params("parallel"),
        name="out_ln",
    )(ret, mla, xf, w, g, b)


def _ffn_ln_kernel(x_ref, wu_ref, wd_ref, g_ref, b_ref, o_ref, xb_ref, acc_ref, *, alpha):
    f = pl.program_id(1)

    @pl.when(f == 0)
    def _():
        xb_ref[...] = x_ref[...].astype(BF16)
        acc_ref[...] = jnp.zeros_like(acc_ref)

    u = jnp.dot(xb_ref[...], wu_ref[...], preferred_element_type=F32)
    a = jnp.square(jnp.maximum(u, 0.0)).astype(BF16)
    acc_ref[...] += jnp.dot(a, wd_ref[...], preferred_element_type=F32)

    @pl.when(f == pl.num_programs(1) - 1)
    def _():
        y = alpha * x_ref[...] + acc_ref[...]
        o_ref[...] = _layer_norm(y, g_ref[...], b_ref[...])


def _ffn_ln(x1, wu, wd, g, b, alpha):
    t, d = x1.shape
    dff = wu.shape[1]
    tm, tf = 512, 1024
    return pl.pallas_call(
        functools.partial(_ffn_ln_kernel, alpha=alpha),
        grid=(t // tm, dff // tf),
        in_specs=[pl.BlockSpec((tm, d), lambda i, f: (i, 0)),
                  pl.BlockSpec((d, tf), lambda i, f: (0, f)),
                  pl.BlockSpec((tf, d), lambda i, f: (f, 0)),
                  pl.BlockSpec((1, d), lambda i, f: (0, 0)),
                  pl.BlockSpec((1, d), lambda i, f: (0, 0))],
        out_specs=pl.BlockSpec((tm, d), lambda i, f: (i, 0)),
        out_shape=jax.ShapeDtypeStruct((t, d), F32),
        scratch_shapes=[pltpu.VMEM((tm, d), BF16), pltpu.VMEM((tm, d), F32)],
        compiler_params=_params("parallel", "arbitrary"),
        name="ffn_ln",
    )(x1, wu, wd, g, b)


def kernel(x, positions, w_in, q_norm_g, w_uq, kv_norm_g, w_uk, w_uv, ret_gn_g, w_out, ln1_g, ln1_b,
           w_up, w_down, ln2_g, ln2_b):
    batch, seq, d = x.shape
    depth = w_in.shape[0]
    t = batch * seq
    alpha = (2.0 * depth) ** 0.25
    xf = x.reshape(t, d)
    c_ret, s_ret, c_mla, sa_mla, sb_mla = _rope_tables(positions.reshape(t, 1))
    log_g = jnp.log1p(-jnp.exp2(-5.0 - jnp.arange(RET_HEADS, dtype=F32)))
    o_cq = 2 * RET_QK + 2 * RET_WIDTH
    o_ckv = o_cq + MLA_Q_RANK
    o_kr = o_ckv + MLA_KV_RANK
    for l in range(depth):
        w = w_in[l].astype(BF16)
        w_cq = w[:, o_cq:o_ckv]
        w_ckv = w[:, o_ckv:o_kr]
        w_kr = jnp.pad(w[:, o_kr:], ((0, 0), (0, LANES - MLA_ROPE)))
        wuq = w_uq[l].reshape(MLA_Q_RANK, MLA_HEADS, MLA_NOPE + MLA_ROPE)
        wuq = jnp.pad(wuq, ((0, 0), (0, 0), (0, MLA_QK_PAD - MLA_NOPE - MLA_ROPE)))
        wuq = wuq.reshape(MLA_Q_RANK, MLA_HEADS * MLA_QK_PAD).astype(BF16)

        qkvg, xb = _proj(xf, w, c_ret, s_ret)
        ret = _retention(qkvg, ret_gn_g[l].reshape(1, RET_WIDTH), log_g, batch, seq)
        q, k, v = _mla_prep(xb, w_cq, w_ckv, w_kr, q_norm_g[l].reshape(1, -1),
                            kv_norm_g[l].reshape(1, -1), wuq, w_uk[l].astype(BF16),
                            w_uv[l].astype(BF16), c_mla, sa_mla, sb_mla)
        mla = _mla_attn(q, k, v, batch, seq)
        x1 = _out_ln(ret, mla, xf, w_out[l].astype(BF16), ln1_g[l].reshape(1, d),
                     ln1_b[l].reshape(1, d), alpha)
        xf = _ffn_ln(x1, w_up[l].astype(BF16), w_down[l].astype(BF16), ln2_g[l].reshape(1, d),
                     ln2_b[l].reshape(1, d), alpha)
    return xf.reshape(batch, seq, d)
```

```python
import functools

import jax
import jax.numpy as jnp
from jax import lax
from jax.experimental import pallas as pl
from jax.experimental.pallas import tpu as pltpu

F32 = jnp.float32
BF16 = jnp.bfloat16

LANES = 128
CHUNK = 64
RET_HEADS = 8
RET_DK = 128
RET_DV = 128
RET_QK = RET_HEADS * RET_DK
RET_WIDTH = RET_HEADS * RET_DV
MLA_HEADS = 8
MLA_NOPE = 128
MLA_ROPE = 64
MLA_DV = 128
MLA_Q_RANK = 768
MLA_KV_RANK = 512
MLA_QK_PAD = 256
MLA_WIDTH = MLA_HEADS * MLA_DV
MLA_Q_SCALE = (MLA_NOPE + MLA_ROPE) ** -0.5 * 1.4426950408889634
ROPE_BASE = 10000.0
EPS = 1e-5

VMEM_LIMIT = 56 * 1024 * 1024
OUT_LN_SPLIT = 2
PROJ_SPLIT = 2


def _params(*sem):
    return pltpu.CompilerParams(dimension_semantics=sem, vmem_limit_bytes=VMEM_LIMIT)


def _rope_table_kernel(pos_ref, cst_ref, cr_ref, sr_ref, cm_ref, sa_ref, sb_ref):
    pos = pos_ref[...].astype(F32)
    ang = pos * cst_ref[0:1, :]
    c = jnp.cos(ang)
    s = jnp.sin(ang)
    half = LANES // 2
    quarter = LANES // 4
    c64 = pltpu.roll(c, half, 1)
    s64 = pltpu.roll(s, half, 1)
    cr_ref[...] = c * cst_ref[1:2, :] + c64 * cst_ref[3:4, :]
    sr_ref[...] = s * cst_ref[2:3, :] + s64 * cst_ref[3:4, :]
    cm_ref[...] = c64 * cst_ref[4:5, :] + pltpu.roll(c, half + quarter, 1) * cst_ref[6:7, :]
    sa_ref[...] = s64 * cst_ref[5:6, :]
    sb_ref[...] = pltpu.roll(s, half + quarter, 1) * cst_ref[6:7, :]


def _rope_tables(pos):
    t = pos.shape[0]
    tm = 2048
    half_r = RET_DK // 2
    half_m = MLA_ROPE // 2
    inv_r = ROPE_BASE ** (-jnp.arange(0, RET_DK, 2, dtype=F32) / RET_DK)
    inv_m = ROPE_BASE ** (-jnp.arange(0, MLA_ROPE, 2, dtype=F32) / MLA_ROPE)
    assert half_r == LANES // 2 and half_m == LANES // 4
    lane = jnp.arange(LANES)
    lo64 = (lane < half_r).astype(F32)
    q0 = (lane < half_m).astype(F32)
    q1 = ((lane >= half_m) & (lane < 2 * half_m)).astype(F32)
    cst = jnp.stack([
        jnp.concatenate([inv_r, inv_m, jnp.zeros((LANES - half_r - half_m,), F32)]),
        lo64, -lo64, 1.0 - lo64, q0, -q0, q1, jnp.zeros((LANES,), F32)])
    out = jax.ShapeDtypeStruct((t, LANES), F32)
    tab_spec = pl.BlockSpec((tm, LANES), lambda i: (i, 0))
    return pl.pallas_call(
        _rope_table_kernel,
        grid=(t // tm,),
        in_specs=[pl.BlockSpec((tm, 1), lambda i: (i, 0)),
                  pl.BlockSpec((8, LANES), lambda i: (0, 0))],
        out_specs=[tab_spec] * 5,
        out_shape=[out] * 5,
        compiler_params=_params("parallel"),
        name="rope_tables",
    )(pos, cst)


def _proj_kernel(x_ref, w_ref, c_ref, s_ref, o_ref, xb_ref, *, k_scale):
    j = pl.program_id(1)

    @pl.when(j == 0)
    def _():
        xb_ref[...] = x_ref[...].astype(xb_ref.dtype)

    rotary = j < 2
    scale = jnp.where(j == 1, k_scale, 1.0).astype(F32)
    rows = x_ref.shape[0] // PROJ_SPLIT
    for r0 in range(0, x_ref.shape[0], rows):
        sl = slice(r0, r0 + rows)
        res = jnp.dot(xb_ref[sl, :], w_ref[...], preferred_element_type=F32)
        c = jnp.where(rotary, c_ref[sl, :] * scale, 1.0)
        s = jnp.where(rotary, s_ref[sl, :] * scale, 0.0)
        for h in range(res.shape[1] // RET_DK):
            r = res[:, h * RET_DK:(h + 1) * RET_DK]
            o = r * c + pltpu.roll(r, RET_DK // 2, 1) * s
            o_ref[sl, h * RET_DK:(h + 1) * RET_DK] = o.astype(o_ref.dtype)


def _proj(xf, w, c_tab, s_tab):
    t, d = xf.shape
    tm, tn = 1024, RET_QK
    n = 4 * tn
    assert RET_QK == RET_WIDTH and w.shape[1] >= n
    return pl.pallas_call(
        functools.partial(_proj_kernel, k_scale=RET_DK ** -0.5),
        grid=(t // tm, n // tn),
        in_specs=[pl.BlockSpec((tm, d), lambda i, j: (i, 0)),
                  pl.BlockSpec((d, tn), lambda i, j: (0, j)),
                  pl.BlockSpec((tm, LANES), lambda i, j: (i, 0)),
                  pl.BlockSpec((tm, LANES), lambda i, j: (i, 0))],
        out_specs=[pl.BlockSpec((tm, tn), lambda i, j: (i, j)),
                   pl.BlockSpec((tm, d), lambda i, j: (i, 0))],
        out_shape=[jax.ShapeDtypeStruct((t, n), BF16), jax.ShapeDtypeStruct((t, d), BF16)],
        compiler_params=_params("parallel", "arbitrary"),
        name="proj",
    )(xf, w, c_tab, s_tab)


def _retention_kernel(lg_ref, q_ref, k_ref, v_ref, g_ref, gn_ref, o_ref, d_ref, *, blk):
    lg = lg_ref[pl.program_id(1)]
    row = lax.broadcasted_iota(jnp.int32, (blk, blk), 0)
    col = lax.broadcasted_iota(jnp.int32, (blk, blk), 1)
    dist = jnp.abs(row - col).astype(F32)
    d_ref[...] = jnp.where(col // CHUNK <= row // CHUNK, jnp.exp(lg * dist), 0.0)
    idx = lax.broadcasted_iota(jnp.int32, (blk, 1), 0).astype(F32)
    q_dec = jnp.exp(lg * (idx + 1.0))
    k_dec = jnp.exp(lg * (blk - 1.0 - idx))
    blk_dec = jnp.exp(jnp.full((1, 1), blk, F32) * lg)
    gn = gn_ref[...]
    nt = (((1,), (1,)), ((), ()))
    tn = (((0,), (0,)), ((), ()))

    def body(n, state):
        sl = pl.ds(pl.multiple_of(n * blk, blk), blk)
        q = q_ref[sl, :]
        k = k_ref[sl, :]
        v = v_ref[sl, :]
        s = lax.dot_general(q, k, nt, preferred_element_type=F32) * d_ref[...]
        intra = jnp.dot(s.astype(BF16), v, preferred_element_type=F32)
        cross = jnp.dot(q, state.astype(BF16), preferred_element_type=F32) * q_dec
        kd = (k.astype(F32) * k_dec).astype(BF16)
        new_state = blk_dec * state + lax.dot_general(kd, v, tn, preferred_element_type=F32)
        ret = intra + cross
        mu = jnp.mean(ret, axis=-1, keepdims=True)
        dev = ret - mu
        var = jnp.mean(dev * dev, axis=-1, keepdims=True)
        y = dev * lax.rsqrt(var + EPS) * gn
        g = g_ref[sl, :].astype(F32)
        gate = g / (1.0 + jnp.exp(-g))
        o_ref[sl, :] = (gate * y).astype(o_ref.dtype)
        return new_state

    lax.fori_loop(0, q_ref.shape[0] // blk, body, jnp.zeros((RET_DK, RET_DV), F32), unroll=True)


def _retention(qkvg, gn_g, log_g, batch, seq):
    t = qkvg.shape[0]
    blk = 256
    h = RET_HEADS
    return pl.pallas_call(
        functools.partial(_retention_kernel, blk=blk),
        grid=(batch, h),
        in_specs=[pl.BlockSpec(memory_space=pltpu.SMEM),
                  pl.BlockSpec((seq, RET_DK), lambda b, i: (b, i)),
                  pl.BlockSpec((seq, RET_DK), lambda b, i: (b, h + i)),
                  pl.BlockSpec((seq, RET_DV), lambda b, i: (b, 2 * h + i)),
                  pl.BlockSpec((seq, RET_DV), lambda b, i: (b, 3 * h + i)),
                  pl.BlockSpec((1, RET_DV), lambda b, i: (0, i))],
        out_specs=pl.BlockSpec((seq, RET_DV), lambda b, i: (b, i)),
        out_shape=jax.ShapeDtypeStruct((t, RET_WIDTH), BF16),
        scratch_shapes=[pltpu.VMEM((blk, blk), F32)],
        compiler_params=_params("parallel", "arbitrary"),
        name="retention",
    )(log_g, qkvg, qkvg, qkvg, qkvg, gn_g)


def _rms(v, g):
    return v * lax.rsqrt(jnp.mean(v * v, axis=-1, keepdims=True) + EPS) * g


def _rope64(r, cm, sa, sb):
    half = MLA_ROPE // 2
    return r * cm + pltpu.roll(r, LANES - half, 1) * sa + pltpu.roll(r, half, 1) * sb


def _mla_prep_kernel(x_ref, wcq_ref, wckv_ref, wkr_ref, qg_ref, kvg_ref, wuq_ref, wuk_ref, wuv_ref,
                     cm_ref, sa_ref, sb_ref, q_ref, k_ref, v_ref):
    x = x_ref[...]
    cm = cm_ref[...]
    sa = sa_ref[...]
    sb = sb_ref[...]
    cq = jnp.dot(x, wcq_ref[...], preferred_element_type=F32)
    cqn = _rms(cq, qg_ref[...]).astype(BF16)
    q = jnp.dot(cqn, wuq_ref[...], preferred_element_type=F32) * MLA_Q_SCALE
    for h in range(MLA_HEADS):
        lo = h * MLA_QK_PAD
        q_ref[:, lo:lo + MLA_NOPE] = q[:, lo:lo + MLA_NOPE].astype(q_ref.dtype)
        q_ref[:, lo + MLA_NOPE:lo + MLA_QK_PAD] = _rope64(
            q[:, lo + MLA_NOPE:lo + MLA_QK_PAD], cm, sa, sb).astype(q_ref.dtype)
    ckv = jnp.dot(x, wckv_ref[...], preferred_element_type=F32)
    ckvn = _rms(ckv, kvg_ref[...]).astype(BF16)
    kr = jnp.dot(x, wkr_ref[...], preferred_element_type=F32)
    k_rope = _rope64(kr, cm, sa, sb).astype(k_ref.dtype)
    kn = jnp.dot(ckvn, wuk_ref[...], preferred_element_type=F32)
    for h in range(MLA_HEADS):
        lo = h * MLA_QK_PAD
        k_ref[:, lo:lo + MLA_NOPE] = kn[:, h * MLA_NOPE:(h + 1) * MLA_NOPE].astype(k_ref.dtype)
        k_ref[:, lo + MLA_NOPE:lo + MLA_QK_PAD] = k_rope
    v_ref[...] = jnp.dot(ckvn, wuv_ref[...], preferred_element_type=F32).astype(v_ref.dtype)


def _mla_prep(xb, wcq, wckv, wkr, qg, kvg, wuq, wuk, wuv, cm, sa, sb):
    t, d = xb.shape
    tm = 1024
    row = lambda i: (i, 0)
    whole = lambda i: (0, 0)
    full = lambda a: pl.BlockSpec(a.shape, whole, pipeline_mode=pl.Buffered(1))
    qk_w = MLA_HEADS * MLA_QK_PAD
    return pl.pallas_call(
        _mla_prep_kernel,
        grid=(t // tm,),
        in_specs=[pl.BlockSpec((tm, d), row), full(wcq), full(wckv), full(wkr), full(qg), full(kvg),
                  full(wuq), full(wuk), full(wuv),
                  pl.BlockSpec((tm, LANES), row), pl.BlockSpec((tm, LANES), row),
                  pl.BlockSpec((tm, LANES), row)],
        out_specs=[pl.BlockSpec((tm, qk_w), row), pl.BlockSpec((tm, qk_w), row),
                   pl.BlockSpec((tm, MLA_WIDTH), row)],
        out_shape=[jax.ShapeDtypeStruct((t, qk_w), BF16), jax.ShapeDtypeStruct((t, qk_w), BF16),
                   jax.ShapeDtypeStruct((t, MLA_WIDTH), BF16)],
        compiler_params=_params("parallel"),
        name="mla_prep",
    )(xb, wcq, wckv, wkr, qg, kvg, wuq, wuk, wuv, cm, sa, sb)


def _mla_attn_kernel(q_ref, k_ref, v_ref, o_ref, sa_ref, sb_ref, pa_ref, pb_ref, acc_ref, qt_ref, *,
                     qblk, kblk):
    tn = (((0,), (0,)), ((), ()))

    def keys(j):
        return pl.ds(pl.multiple_of(j * kblk, kblk), kblk)

    def scores(s_ref, j):
        s_ref[...] = jnp.dot(k_ref[keys(j), :], qt_ref[...], preferred_element_type=F32)

    def probs(s_ref, p_ref, stats, key_off=None):
        m, l = stats
        st = s_ref[...]
        if key_off is not None:
            key = lax.broadcasted_iota(jnp.int32, st.shape, 0) + key_off
            qry = lax.broadcasted_iota(jnp.int32, st.shape, 1)
            st = jnp.where(key // CHUNK <= qry // CHUNK, st, -jnp.inf)
        m_new = jnp.maximum(m, jnp.max(st, axis=0, keepdims=True))
        alpha = jnp.exp2(m - m_new)
        p = jnp.exp2(st - m_new)
        p_ref[...] = p.astype(p_ref.dtype)
        return (m_new, alpha * l + jnp.sum(p, axis=0, keepdims=True)), alpha

    def accum(p_ref, j, alpha):
        pv = lax.dot_general(v_ref[keys(j), :], p_ref[...], tn, preferred_element_type=F32)
        acc_ref[...] = alpha * acc_ref[...] + pv

    def body(jj, carry):
        stats, alpha = carry
        j = 2 * jj
        scores(sb_ref, j + 1)
        accum(pb_ref, jnp.maximum(j - 1, 0), alpha)
        stats, alpha = probs(sa_ref, pa_ref, stats)
        scores(sa_ref, j + 2)
        accum(pa_ref, j, alpha)
        stats, alpha = probs(sb_ref, pb_ref, stats)
        return stats, alpha

    nq = q_ref.shape[0] // qblk

    def query_rows(qi):
        return pl.ds(pl.multiple_of(qi * qblk, qblk), qblk)

    def start_query_block(qi):
        qt_ref[...] = q_ref[query_rows(qi), :].T
        scores(sa_ref, 0)

    def query_block(qi, _):
        acc_ref[...] = jnp.zeros_like(acc_ref)
        pb_ref[...] = jnp.zeros_like(pb_ref)
        stats = (jnp.full((1, qblk), -jnp.inf, F32), jnp.zeros((1, qblk), F32))
        nfull = qi * (qblk // kblk)
        stats, alpha = lax.fori_loop(0, nfull // 2, body, (stats, jnp.ones((1, qblk), F32)))
        scores(sb_ref, nfull + 1)
        accum(pb_ref, jnp.maximum(nfull - 1, 0), alpha)
        stats, alpha = probs(sa_ref, pa_ref, stats, key_off=0)
        start_query_block(jnp.minimum(qi + 1, nq - 1))
        accum(pa_ref, nfull, alpha)
        (m, l), alpha = probs(sb_ref, pb_ref, stats, key_off=kblk)
        accum(pb_ref, nfull + 1, alpha)
        o_ref[query_rows(qi), :] = (acc_ref[...] / l).T.astype(o_ref.dtype)
        return 0

    start_query_block(0)
    lax.fori_loop(0, nq, query_block, 0)


def _mla_attn(q, k, v, batch, seq):
    t = q.shape[0]
    qblk = 1024
    kblk = qblk // 2
    per_head = lambda b, h: (b, h)
    return pl.pallas_call(
        functools.partial(_mla_attn_kernel, qblk=qblk, kblk=kblk),
        grid=(batch, MLA_HEADS),
        in_specs=[pl.BlockSpec((seq, MLA_QK_PAD), per_head),
                  pl.BlockSpec((seq, MLA_QK_PAD), per_head),
                  pl.BlockSpec((seq, MLA_DV), per_head)],
        out_specs=pl.BlockSpec((seq, MLA_DV), per_head),
        out_shape=jax.ShapeDtypeStruct((t, MLA_WIDTH), BF16),
        scratch_shapes=[pltpu.VMEM((kblk, qblk), F32), pltpu.VMEM((kblk, qblk), F32),
                        pltpu.VMEM((kblk, qblk), BF16), pltpu.VMEM((kblk, qblk), BF16),
                        pltpu.VMEM((MLA_DV, qblk), F32), pltpu.VMEM((MLA_QK_PAD, qblk), BF16)],
        compiler_params=_params("parallel", "arbitrary"),
        name="mla_attn",
    )(q, k, v)


def _layer_norm(y, g, b):
    mu = jnp.mean(y, axis=-1, keepdims=True)
    dev = y - mu
    var = jnp.mean(dev * dev, axis=-1, keepdims=True)
    return dev * lax.rsqrt(var + EPS) * g + b


def _out_ln_kernel(ret_ref, mla_ref, x_ref, w_ref, g_ref, b_ref, o_ref, ob_ref, *, alpha):
    kr = ret_ref.shape[1]
    rows = x_ref.shape[0] // OUT_LN_SPLIT
    for r in range(OUT_LN_SPLIT):
        sl = slice(r * rows, (r + 1) * rows)
        mix = jnp.dot(ret_ref[sl, :], w_ref[0:kr, :], preferred_element_type=F32)
        mix = mix + jnp.dot(mla_ref[sl, :], w_ref[kr:, :], preferred_element_type=F32)
        y = alpha * x_ref[sl, :] + mix
        out = _layer_norm(y, g_ref[...], b_ref[...])
        o_ref[sl, :] = out
        ob_ref[sl, :] = out.astype(ob_ref.dtype)


def _out_ln(ret, mla, xf, w, g, b, alpha):
    t, d = xf.shape
    tm = 512
    row = lambda i: (i, 0)
    whole = lambda i: (0, 0)
    return pl.pallas_call(
        functools.partial(_out_ln_kernel, alpha=alpha),
        grid=(t // tm,),
        in_specs=[pl.BlockSpec((tm, ret.shape[1]), row), pl.BlockSpec((tm, mla.shape[1]), row),
                  pl.BlockSpec((tm, d), row), pl.BlockSpec(w.shape, whole),
                  pl.BlockSpec((1, d), whole), pl.BlockSpec((1, d), whole)],
        out_specs=[pl.BlockSpec((tm, d), row), pl.BlockSpec((tm, d), row)],
        out_shape=[jax.ShapeDtypeStruct((t, d), F32), jax.ShapeDtypeStruct((t, d), BF16)],
        compiler_params=_params("parallel"),
        name="out_ln",
    )(ret, mla, xf, w, g, b)


def _ffn_ln_kernel(x_ref, xb_ref, wu_ref, wd_ref, g_ref, b_ref, o_ref, acc_ref, y_ref, *, alpha, nf):
    t = pl.program_id(0)

    def mlp():
        u = jnp.dot(xb_ref[...], wu_ref[...], preferred_element_type=F32)
        a = jnp.square(jnp.maximum(u, 0.0)).astype(BF16)
        return jnp.dot(a, wd_ref[...], preferred_element_type=F32)

    @pl.when(t == 0)
    def _():
        acc_ref[...] = jnp.zeros_like(acc_ref)

    @pl.when(t % nf == 0)
    def _():
        y_ref[...] = alpha * x_ref[...] + acc_ref[...]
        o_ref[...] = _layer_norm(y_ref[...], g_ref[...], b_ref[...])
        acc_ref[...] = mlp()

    @pl.when(t % nf != 0)
    def _():
        acc_ref[...] += mlp()


def _ffn_ln(x1, x1b, wu, wd, g, b, alpha):
    t, d = x1.shape
    dff = wu.shape[1]
    tm, tf = 512, 1024
    nt, nf = t // tm, dff // tf
    last = nt * nf - 1
    rows_now = lambda s: (jnp.minimum(s, last) // nf, 0)
    rows_late = lambda s: (jnp.maximum(s - 1, 0) // nf, 0)
    hidden = lambda s: jnp.minimum(s, last) % nf
    return pl.pallas_call(
        functools.partial(_ffn_ln_kernel, alpha=alpha, nf=nf),
        grid=(nt * nf + 1,),
        in_specs=[pl.BlockSpec((tm, d), rows_late),
                  pl.BlockSpec((tm, d), rows_now),
                  pl.BlockSpec((d, tf), lambda s: (0, hidden(s))),
                  pl.BlockSpec((tf, d), lambda s: (hidden(s), 0)),
                  pl.BlockSpec((1, d), lambda s: (0, 0)),
                  pl.BlockSpec((1, d), lambda s: (0, 0))],
        out_specs=pl.BlockSpec((tm, d), rows_late),
        out_shape=jax.ShapeDtypeStruct((t, d), F32),
        scratch_shapes=[pltpu.VMEM((tm, d), F32), pltpu.VMEM((tm, d), F32)],
        compiler_params=_params("arbitrary"),
        name="ffn_ln",
    )(x1, x1b, wu, wd, g, b)


def kernel(x, positions, w_in, q_norm_g, w_uq, kv_norm_g, w_uk, w_uv, ret_gn_g, w_out, ln1_g, ln1_b,
           w_up, w_down, ln2_g, ln2_b):
    batch, seq, d = x.shape
    depth = w_in.shape[0]
    t = batch * seq
    alpha = (2.0 * depth) ** 0.25
    xf = x.reshape(t, d)
    c_ret, s_ret, c_mla, sa_mla, sb_mla = _rope_tables(positions.reshape(t, 1))
    log_g = jnp.log1p(-jnp.exp2(-5.0 - jnp.arange(RET_HEADS, dtype=F32)))
    o_cq = 2 * RET_QK + 2 * RET_WIDTH
    o_ckv = o_cq + MLA_Q_RANK
    o_kr = o_ckv + MLA_KV_RANK
    for l in range(depth):
        w = w_in[l].astype(BF16)
        w_cq = w[:, o_cq:o_ckv]
        w_ckv = w[:, o_ckv:o_kr]
        w_kr = jnp.pad(w[:, o_kr:], ((0, 0), (0, LANES - MLA_ROPE)))
        wuq = w_uq[l].reshape(MLA_Q_RANK, MLA_HEADS, MLA_NOPE + MLA_ROPE)
        wuq = jnp.pad(wuq, ((0, 0), (0, 0), (0, MLA_QK_PAD - MLA_NOPE - MLA_ROPE)))
        wuq = wuq.reshape(MLA_Q_RANK, MLA_HEADS * MLA_QK_PAD).astype(BF16)

        qkvg, xb = _proj(xf, w, c_ret, s_ret)
        ret = _retention(qkvg, ret_gn_g[l].reshape(1, RET_WIDTH), log_g, batch, seq)
        q, k, v = _mla_prep(xb, w_cq, w_ckv, w_kr, q_norm_g[l].reshape(1, -1),
                            kv_norm_g[l].reshape(1, -1), wuq, w_uk[l].astype(BF16),
                            w_uv[l].astype(BF16), c_mla, sa_mla, sb_mla)
        mla = _mla_attn(q, k, v, batch, seq)
        x1, x1b = _out_ln(ret, mla, xf, w_out[l].astype(BF16), ln1_g[l].reshape(1, d),
                     ln1_b[l].reshape(1, d), alpha)
        xf = _ffn_ln(x1, x1b, w_up[l].astype(BF16), w_down[l].astype(BF16), ln2_g[l].reshape(1, d),
                     ln2_b[l].reshape(1, d), alpha)
    return xf.reshape(batch, seq, d)
```

```python
import functools

import jax
import jax.numpy as jnp
from jax import lax
from jax.experimental import pallas as pl
from jax.experimental.pallas import tpu as pltpu

F32 = jnp.float32
BF16 = jnp.bfloat16

LANES = 128
CHUNK = 64
RET_HEADS = 8
RET_DK = 128
RET_DV = 128
RET_QK = RET_HEADS * RET_DK
RET_WIDTH = RET_HEADS * RET_DV
MLA_HEADS = 8
MLA_NOPE = 128
MLA_ROPE = 64
MLA_DV = 128
MLA_Q_RANK = 768
MLA_KV_RANK = 512
MLA_QK_PAD = 256
MLA_WIDTH = MLA_HEADS * MLA_DV
MLA_Q_SCALE = (MLA_NOPE + MLA_ROPE) ** -0.5 * 1.4426950408889634
ROPE_BASE = 10000.0
EPS = 1e-5

VMEM_LIMIT = 56 * 1024 * 1024
OUT_LN_SPLIT = 2
PROJ_SPLIT = 2


def _params(*sem):
    return pltpu.CompilerParams(dimension_semantics=sem, vmem_limit_bytes=VMEM_LIMIT)


def _rope_table_kernel(pos_ref, cst_ref, cr_ref, sr_ref, cm_ref, sa_ref, sb_ref):
    pos = pos_ref[...].astype(F32)
    ang = pos * cst_ref[0:1, :]
    c = jnp.cos(ang)
    s = jnp.sin(ang)
    half = LANES // 2
    quarter = LANES // 4
    c64 = pltpu.roll(c, half, 1)
    s64 = pltpu.roll(s, half, 1)
    cr_ref[...] = c * cst_ref[1:2, :] + c64 * cst_ref[3:4, :]
    sr_ref[...] = s * cst_ref[2:3, :] + s64 * cst_ref[3:4, :]
    cm_ref[...] = c64 * cst_ref[4:5, :] + pltpu.roll(c, half + quarter, 1) * cst_ref[6:7, :]
    sa_ref[...] = s64 * cst_ref[5:6, :]
    sb_ref[...] = pltpu.roll(s, half + quarter, 1) * cst_ref[6:7, :]


def _rope_tables(pos):
    t = pos.shape[0]
    tm = 2048
    half_r = RET_DK // 2
    half_m = MLA_ROPE // 2
    inv_r = ROPE_BASE ** (-jnp.arange(0, RET_DK, 2, dtype=F32) / RET_DK)
    inv_m = ROPE_BASE ** (-jnp.arange(0, MLA_ROPE, 2, dtype=F32) / MLA_ROPE)
    assert half_r == LANES // 2 and half_m == LANES // 4
    lane = jnp.arange(LANES)
    lo64 = (lane < half_r).astype(F32)
    q0 = (lane < half_m).astype(F32)
    q1 = ((lane >= half_m) & (lane < 2 * half_m)).astype(F32)
    cst = jnp.stack([
        jnp.concatenate([inv_r, inv_m, jnp.zeros((LANES - half_r - half_m,), F32)]),
        lo64, -lo64, 1.0 - lo64, q0, -q0, q1, jnp.zeros((LANES,), F32)])
    out = jax.ShapeDtypeStruct((t, LANES), F32)
    tab_spec = pl.BlockSpec((tm, LANES), lambda i: (i, 0))
    return pl.pallas_call(
        _rope_table_kernel,
        grid=(t // tm,),
        in_specs=[pl.BlockSpec((tm, 1), lambda i: (i, 0)),
                  pl.BlockSpec((8, LANES), lambda i: (0, 0))],
        out_specs=[tab_spec] * 5,
        out_shape=[out] * 5,
        compiler_params=_params("parallel"),
        name="rope_tables",
    )(pos, cst)


def _proj_kernel(x_ref, w_ref, c_ref, s_ref, o_ref, xb_ref, *, k_scale):
    j = pl.program_id(1)

    @pl.when(j == 0)
    def _():
        xb_ref[...] = x_ref[...].astype(xb_ref.dtype)

    rotary = j < 2
    scale = jnp.where(j == 1, k_scale, 1.0).astype(F32)
    rows = x_ref.shape[0] // PROJ_SPLIT
    for r0 in range(0, x_ref.shape[0], rows):
        sl = slice(r0, r0 + rows)
        res = jnp.dot(xb_ref[sl, :], w_ref[...], preferred_element_type=F32)
        c = jnp.where(rotary, c_ref[sl, :] * scale, 1.0)
        s = jnp.where(rotary, s_ref[sl, :] * scale, 0.0)
        for h in range(res.shape[1] // RET_DK):
            r = res[:, h * RET_DK:(h + 1) * RET_DK]
            o = r * c + pltpu.roll(r, RET_DK // 2, 1) * s
            o_ref[sl, h * RET_DK:(h + 1) * RET_DK] = o.astype(o_ref.dtype)


def _proj(xf, w, c_tab, s_tab):
    t, d = xf.shape
    tm, tn = 1024, RET_QK
    n = 4 * tn
    assert RET_QK == RET_WIDTH and w.shape[1] >= n
    return pl.pallas_call(
        functools.partial(_proj_kernel, k_scale=RET_DK ** -0.5),
        grid=(t // tm, n // tn),
        in_specs=[pl.BlockSpec((tm, d), lambda i, j: (i, 0)),
                  pl.BlockSpec((d, tn), lambda i, j: (0, j)),
                  pl.BlockSpec((tm, LANES), lambda i, j: (i, 0)),
                  pl.BlockSpec((tm, LANES), lambda i, j: (i, 0))],
        out_specs=[pl.BlockSpec((tm, tn), lambda i, j: (i, j)),
                   pl.BlockSpec((tm, d), lambda i, j: (i, 0))],
        out_shape=[jax.ShapeDtypeStruct((t, n), BF16), jax.ShapeDtypeStruct((t, d), BF16)],
        compiler_params=_params("parallel", "arbitrary"),
        name="proj",
    )(xf, w, c_tab, s_tab)


def _retention_kernel(lg_ref, q_ref, k_ref, v_ref, g_ref, gn_ref, o_ref, d_ref, *, blk):
    lg = lg_ref[pl.program_id(1)]
    row = lax.broadcasted_iota(jnp.int32, (blk, blk), 0)
    col = lax.broadcasted_iota(jnp.int32, (blk, blk), 1)
    dist = jnp.abs(row - col).astype(F32)
    d_ref[...] = jnp.where(col // CHUNK <= row // CHUNK, jnp.exp(lg * dist), 0.0)
    idx = lax.broadcasted_iota(jnp.int32, (blk, 1), 0).astype(F32)
    q_dec = jnp.exp(lg * (idx + 1.0))
    k_dec = jnp.exp(lg * (blk - 1.0 - idx))
    blk_dec = jnp.exp(jnp.full((1, 1), blk, F32) * lg)
    gn = gn_ref[...]
    nt = (((1,), (1,)), ((), ()))
    tn = (((0,), (0,)), ((), ()))

    def body(n, state):
        sl = pl.ds(pl.multiple_of(n * blk, blk), blk)
        q = q_ref[sl, :]
        k = k_ref[sl, :]
        v = v_ref[sl, :]
        s = lax.dot_general(q, k, nt, preferred_element_type=F32) * d_ref[...]
        intra = jnp.dot(s.astype(BF16), v, preferred_element_type=F32)
        cross = jnp.dot(q, state.astype(BF16), preferred_element_type=F32) * q_dec
        kd = (k.astype(F32) * k_dec).astype(BF16)
        new_state = blk_dec * state + lax.dot_general(kd, v, tn, preferred_element_type=F32)
        ret = intra + cross
        mu = jnp.mean(ret, axis=-1, keepdims=True)
        dev = ret - mu
        var = jnp.mean(dev * dev, axis=-1, keepdims=True)
        y = dev * lax.rsqrt(var + EPS) * gn
        g = g_ref[sl, :].astype(F32)
        gate = g / (1.0 + jnp.exp(-g))
        o_ref[sl, :] = (gate * y).astype(o_ref.dtype)
        return new_state

    lax.fori_loop(0, q_ref.shape[0] // blk, body, jnp.zeros((RET_DK, RET_DV), F32), unroll=True)


def _retention(qkvg, gn_g, log_g, batch, seq):
    t = qkvg.shape[0]
    blk = 256
    h = RET_HEADS
    return pl.pallas_call(
        functools.partial(_retention_kernel, blk=blk),
        grid=(batch, h),
        in_specs=[pl.BlockSpec(memory_space=pltpu.SMEM),
                  pl.BlockSpec((seq, RET_DK), lambda b, i: (b, i)),
                  pl.BlockSpec((seq, RET_DK), lambda b, i: (b, h + i)),
                  pl.BlockSpec((seq, RET_DV), lambda b, i: (b, 2 * h + i)),
                  pl.BlockSpec((seq, RET_DV), lambda b, i: (b, 3 * h + i)),
                  pl.BlockSpec((1, RET_DV), lambda b, i: (0, i))],
        out_specs=pl.BlockSpec((seq, RET_DV), lambda b, i: (b, i)),
        out_shape=jax.ShapeDtypeStruct((t, RET_WIDTH), BF16),
        scratch_shapes=[pltpu.VMEM((blk, blk), F32)],
        compiler_params=_params("parallel", "arbitrary"),
        name="retention",
    )(log_g, qkvg, qkvg, qkvg, qkvg, gn_g)


def _rms(v, g):
    return v * lax.rsqrt(jnp.mean(v * v, axis=-1, keepdims=True) + EPS) * g


def _rope64(r, cm, sa, sb):
    half = MLA_ROPE // 2
    return r * cm + pltpu.roll(r, LANES - half, 1) * sa + pltpu.roll(r, half, 1) * sb


def _mla_prep_kernel(x_ref, wcq_ref, wckv_ref, wkr_ref, qg_ref, kvg_ref, wuq_ref, wuk_ref, wuv_ref,
                     cm_ref, sa_ref, sb_ref, q_ref, k_ref, v_ref):
    x = x_ref[...]
    cm = cm_ref[...]
    sa = sa_ref[...]
    sb = sb_ref[...]
    cq = jnp.dot(x, wcq_ref[...], preferred_element_type=F32)
    cqn = _rms(cq, qg_ref[...]).astype(BF16)
    q = jnp.dot(cqn, wuq_ref[...], preferred_element_type=F32) * MLA_Q_SCALE
    rope0 = MLA_HEADS * MLA_NOPE
    for h in range(MLA_HEADS):
        lo = h * MLA_QK_PAD
        q_ref[:, lo:lo + MLA_NOPE] = q[:, h * MLA_NOPE:(h + 1) * MLA_NOPE].astype(q_ref.dtype)
        pair = q[:, rope0 + (h // 2) * LANES:rope0 + (h // 2 + 1) * LANES]
        if h % 2:
            pair = pltpu.roll(pair, LANES // 2, 1)
        q_ref[:, lo + MLA_NOPE:lo + MLA_QK_PAD] = _rope64(pair, cm, sa, sb).astype(q_ref.dtype)
    ckv = jnp.dot(x, wckv_ref[...], preferred_element_type=F32)
    ckvn = _rms(ckv, kvg_ref[...]).astype(BF16)
    kr = jnp.dot(x, wkr_ref[...], preferred_element_type=F32)
    k_rope = _rope64(kr, cm, sa, sb).astype(k_ref.dtype)
    kn = jnp.dot(ckvn, wuk_ref[...], preferred_element_type=F32)
    for h in range(MLA_HEADS):
        lo = h * MLA_QK_PAD
        k_ref[:, lo:lo + MLA_NOPE] = kn[:, h * MLA_NOPE:(h + 1) * MLA_NOPE].astype(k_ref.dtype)
        k_ref[:, lo + MLA_NOPE:lo + MLA_QK_PAD] = k_rope
    v_ref[...] = jnp.dot(ckvn, wuv_ref[...], preferred_element_type=F32).astype(v_ref.dtype)


def _mla_prep(xb, wcq, wckv, wkr, qg, kvg, wuq, wuk, wuv, cm, sa, sb):
    t, d = xb.shape
    tm = 1024
    row = lambda i: (i, 0)
    whole = lambda i: (0, 0)
    full = lambda a: pl.BlockSpec(a.shape, whole, pipeline_mode=pl.Buffered(1))
    qk_w = MLA_HEADS * MLA_QK_PAD
    return pl.pallas_call(
        _mla_prep_kernel,
        grid=(t // tm,),
        in_specs=[pl.BlockSpec((tm, d), row), full(wcq), full(wckv), full(wkr), full(qg), full(kvg),
                  full(wuq), full(wuk), full(wuv),
                  pl.BlockSpec((tm, LANES), row), pl.BlockSpec((tm, LANES), row),
                  pl.BlockSpec((tm, LANES), row)],
        out_specs=[pl.BlockSpec((tm, qk_w), row), pl.BlockSpec((tm, qk_w), row),
                   pl.BlockSpec((tm, MLA_WIDTH), row)],
        out_shape=[jax.ShapeDtypeStruct((t, qk_w), BF16), jax.ShapeDtypeStruct((t, qk_w), BF16),
                   jax.ShapeDtypeStruct((t, MLA_WIDTH), BF16)],
        compiler_params=_params("parallel"),
        name="mla_prep",
    )(xb, wcq, wckv, wkr, qg, kvg, wuq, wuk, wuv, cm, sa, sb)


def _mla_attn_kernel(q_ref, k_ref, v_ref, o_ref, sa_ref, sb_ref, pa_ref, pb_ref, acc_ref, qt_ref, *,
                     qblk, kblk):
    tn = (((0,), (0,)), ((), ()))

    def keys(j):
        return pl.ds(pl.multiple_of(j * kblk, kblk), kblk)

    everything = slice(None)

    def scores(s_ref, j, cols=everything):
        s_ref[:, cols] = jnp.dot(k_ref[keys(j), :], qt_ref[:, cols], preferred_element_type=F32)

    def probs(s_ref, p_ref, stats, cols=everything, masked=False):
        m, l = stats
        st = s_ref[:, cols]
        if masked:
            key = lax.broadcasted_iota(jnp.int32, st.shape, 0)
            qry = lax.broadcasted_iota(jnp.int32, st.shape, 1)
            st = jnp.where(key // CHUNK <= qry // CHUNK, st, -jnp.inf)
        m_new = jnp.maximum(m, jnp.max(st, axis=0, keepdims=True))
        alpha = jnp.exp2(m - m_new)
        p = jnp.exp2(st - m_new)
        p_ref[:, cols] = p.astype(p_ref.dtype)
        return (m_new, alpha * l + jnp.sum(p, axis=0, keepdims=True)), alpha

    def accum(p_ref, j, alpha, cols=everything):
        pv = lax.dot_general(v_ref[keys(j), :], p_ref[:, cols], tn, preferred_element_type=F32)
        acc_ref[:, cols] = alpha * acc_ref[:, cols] + pv

    def body(jj, carry):
        stats, alpha = carry
        j = 2 * jj
        scores(sb_ref, j + 1)
        accum(pb_ref, jnp.maximum(j - 1, 0), alpha)
        stats, alpha = probs(sa_ref, pa_ref, stats)
        scores(sa_ref, j + 2)
        accum(pa_ref, j, alpha)
        stats, alpha = probs(sb_ref, pb_ref, stats)
        return stats, alpha

    nq = q_ref.shape[0] // qblk

    def query_rows(qi):
        return pl.ds(pl.multiple_of(qi * qblk, qblk), qblk)

    def start_query_block(qi):
        qt_ref[...] = q_ref[query_rows(qi), :].T
        scores(sa_ref, 0)

    def query_block(qi, _):
        acc_ref[...] = jnp.zeros_like(acc_ref)
        pb_ref[...] = jnp.zeros_like(pb_ref)
        stats = (jnp.full((1, qblk), -jnp.inf, F32), jnp.zeros((1, qblk), F32))
        nfull = qi * (qblk // kblk)
        stats, alpha = lax.fori_loop(0, nfull // 2, body, (stats, jnp.ones((1, qblk), F32)))
        late = slice(kblk, qblk)
        scores(sb_ref, nfull + 1, late)
        accum(pb_ref, jnp.maximum(nfull - 1, 0), alpha)
        (m, l), alpha = probs(sa_ref, pa_ref, stats, masked=True)
        start_query_block(jnp.minimum(qi + 1, nq - 1))
        accum(pa_ref, nfull, alpha)
        (_, l_late), alpha = probs(sb_ref, pb_ref, (m[:, late], l[:, late]), late, masked=True)
        accum(pb_ref, nfull + 1, alpha, late)
        l = jnp.concatenate([l[:, :kblk], l_late], axis=1)
        o_ref[query_rows(qi), :] = (acc_ref[...] / l).T.astype(o_ref.dtype)
        return 0

    start_query_block(0)
    lax.fori_loop(0, nq, query_block, 0)


def _mla_attn(q, k, v, batch, seq):
    t = q.shape[0]
    qblk = 1024
    kblk = qblk // 2
    per_head = lambda b, h: (b, h)
    return pl.pallas_call(
        functools.partial(_mla_attn_kernel, qblk=qblk, kblk=kblk),
        grid=(batch, MLA_HEADS),
        in_specs=[pl.BlockSpec((seq, MLA_QK_PAD), per_head),
                  pl.BlockSpec((seq, MLA_QK_PAD), per_head),
                  pl.BlockSpec((seq, MLA_DV), per_head)],
        out_specs=pl.BlockSpec((seq, MLA_DV), per_head),
        out_shape=jax.ShapeDtypeStruct((t, MLA_WIDTH), BF16),
        scratch_shapes=[pltpu.VMEM((kblk, qblk), F32), pltpu.VMEM((kblk, qblk), F32),
                        pltpu.VMEM((kblk, qblk), BF16), pltpu.VMEM((kblk, qblk), BF16),
                        pltpu.VMEM((MLA_DV, qblk), F32), pltpu.VMEM((MLA_QK_PAD, qblk), BF16)],
        compiler_params=_params("parallel", "arbitrary"),
        name="mla_attn",
    )(q, k, v)


def _layer_norm(y, g, b):
    mu = jnp.mean(y, axis=-1, keepdims=True)
    dev = y - mu
    var = jnp.mean(dev * dev, axis=-1, keepdims=True)
    return dev * lax.rsqrt(var + EPS) * g + b


def _out_ln_kernel(ret_ref, mla_ref, x_ref, w_ref, g_ref, b_ref, o_ref, ob_ref, *, alpha):
    kr = ret_ref.shape[1]
    rows = x_ref.shape[0] // OUT_LN_SPLIT
    for r in range(OUT_LN_SPLIT):
        sl = slice(r * rows, (r + 1) * rows)
        mix = jnp.dot(ret_ref[sl, :], w_ref[0:kr, :], preferred_element_type=F32)
        mix = mix + jnp.dot(mla_ref[sl, :], w_ref[kr:, :], preferred_element_type=F32)
        y = alpha * x_ref[sl, :] + mix
        out = _layer_norm(y, g_ref[...], b_ref[...])
        o_ref[sl, :] = out
        ob_ref[sl, :] = out.astype(ob_ref.dtype)


def _out_ln(ret, mla, xf, w, g, b, alpha):
    t, d = xf.shape
    tm = 512
    row = lambda i: (i, 0)
    whole = lambda i: (0, 0)
    return pl.pallas_call(
        functools.partial(_out_ln_kernel, alpha=alpha),
        grid=(t // tm,),
        in_specs=[pl.BlockSpec((tm, ret.shape[1]), row), pl.BlockSpec((tm, mla.shape[1]), row),
                  pl.BlockSpec((tm, d), row), pl.BlockSpec(w.shape, whole),
                  pl.BlockSpec((1, d), whole), pl.BlockSpec((1, d), whole)],
        out_specs=[pl.BlockSpec((tm, d), row), pl.BlockSpec((tm, d), row)],
        out_shape=[jax.ShapeDtypeStruct((t, d), F32), jax.ShapeDtypeStruct((t, d), BF16)],
        compiler_params=_params("parallel"),
        name="out_ln",
    )(ret, mla, xf, w, g, b)


def _ffn_ln_kernel(x_ref, xb_ref, wu_ref, wd_ref, g_ref, b_ref, o_ref, acc_ref, y_ref, *, alpha, nf):
    t = pl.program_id(0)

    def mlp():
        u = jnp.dot(xb_ref[...], wu_ref[...], preferred_element_type=F32)
        a = jnp.square(jnp.maximum(u, 0.0)).astype(BF16)
        return jnp.dot(a, wd_ref[...], preferred_element_type=F32)

    @pl.when(t == 0)
    def _():
        acc_ref[...] = jnp.zeros_like(acc_ref)

    @pl.when(t % nf == 0)
    def _():
        y_ref[...] = alpha * x_ref[...] + acc_ref[...]
        o_ref[...] = _layer_norm(y_ref[...], g_ref[...], b_ref[...])
        acc_ref[...] = mlp()

    @pl.when(t % nf != 0)
    def _():
        acc_ref[...] += mlp()


def _ffn_ln(x1, x1b, wu, wd, g, b, alpha):
    t, d = x1.shape
    dff = wu.shape[1]
    tm, tf = 512, 1024
    nt, nf = t // tm, dff // tf
    last = nt * nf - 1
    rows_now = lambda s: (jnp.minimum(s, last) // nf, 0)
    rows_late = lambda s: (jnp.maximum(s - 1, 0) // nf, 0)
    hidden = lambda s: jnp.minimum(s, last) % nf
    return pl.pallas_call(
        functools.partial(_ffn_ln_kernel, alpha=alpha, nf=nf),
        grid=(nt * nf + 1,),
        in_specs=[pl.BlockSpec((tm, d), rows_late),
                  pl.BlockSpec((tm, d), rows_now),
                  pl.BlockSpec((d, tf), lambda s: (0, hidden(s))),
                  pl.BlockSpec((tf, d), lambda s: (hidden(s), 0)),
                  pl.BlockSpec((1, d), lambda s: (0, 0)),
                  pl.BlockSpec((1, d), lambda s: (0, 0))],
        out_specs=pl.BlockSpec((tm, d), rows_late),
        out_shape=jax.ShapeDtypeStruct((t, d), F32),
        scratch_shapes=[pltpu.VMEM((tm, d), F32), pltpu.VMEM((tm, d), F32)],
        compiler_params=_params("arbitrary"),
        name="ffn_ln",
    )(x1, x1b, wu, wd, g, b)


def kernel(x, positions, w_in, q_norm_g, w_uq, kv_norm_g, w_uk, w_uv, ret_gn_g, w_out, ln1_g, ln1_b,
           w_up, w_down, ln2_g, ln2_b):
    batch, seq, d = x.shape
    depth = w_in.shape[0]
    t = batch * seq
    alpha = (2.0 * depth) ** 0.25
    xf = x.reshape(t, d)
    c_ret, s_ret, c_mla, sa_mla, sb_mla = _rope_tables(positions.reshape(t, 1))
    log_g = jnp.log1p(-jnp.exp2(-5.0 - jnp.arange(RET_HEADS, dtype=F32)))
    o_cq = 2 * RET_QK + 2 * RET_WIDTH
    o_ckv = o_cq + MLA_Q_RANK
    o_kr = o_ckv + MLA_KV_RANK
    for l in range(depth):
        w = w_in[l].astype(BF16)
        w_cq = w[:, o_cq:o_ckv]
        w_ckv = w[:, o_ckv:o_kr]
        w_kr = jnp.pad(w[:, o_kr:], ((0, 0), (0, LANES - MLA_ROPE)))
        wuq = w_uq[l].reshape(MLA_Q_RANK, MLA_HEADS, MLA_NOPE + MLA_ROPE)
        wuq = jnp.concatenate([wuq[:, :, :MLA_NOPE].reshape(MLA_Q_RANK, -1),
                               wuq[:, :, MLA_NOPE:].reshape(MLA_Q_RANK, -1)], axis=1).astype(BF16)

        qkvg, xb = _proj(xf, w, c_ret, s_ret)
        ret = _retention(qkvg, ret_gn_g[l].reshape(1, RET_WIDTH), log_g, batch, seq)
        q, k, v = _mla_prep(xb, w_cq, w_ckv, w_kr, q_norm_g[l].reshape(1, -1),
                            kv_norm_g[l].reshape(1, -1), wuq, w_uk[l].astype(BF16),
                            w_uv[l].astype(BF16), c_mla, sa_mla, sb_mla)
        mla = _mla_attn(q, k, v, batch, seq)
        x1, x1b = _out_ln(ret, mla, xf, w_out[l].astype(BF16), ln1_g[l].reshape(1, d),
                     ln1_b[l].reshape(1, d), alpha)
        xf = _ffn_ln(x1, x1b, w_up[l].astype(BF16), w_down[l].astype(BF16), ln2_g[l].reshape(1, d),
                     ln2_b[l].reshape(1, d), alpha)
    return xf.reshape(batch, seq, d)
```

```python
import functools

import jax
import jax.numpy as jnp
from jax import lax
from jax.experimental import pallas as pl
from jax.experimental.pallas import tpu as pltpu

F32 = jnp.float32
BF16 = jnp.bfloat16

LANES = 128
CHUNK = 64
RET_HEADS = 8
RET_DK = 128
RET_DV = 128
RET_QK = RET_HEADS * RET_DK
RET_WIDTH = RET_HEADS * RET_DV
MLA_HEADS = 8
MLA_NOPE = 128
MLA_ROPE = 64
MLA_DV = 128
MLA_Q_RANK = 768
MLA_KV_RANK = 512
MLA_QK_PAD = 256
MLA_WIDTH = MLA_HEADS * MLA_DV
MLA_Q_SCALE = (MLA_NOPE + MLA_ROPE) ** -0.5 * 1.4426950408889634
ROPE_BASE = 10000.0
EPS = 1e-5

VMEM_LIMIT = 56 * 1024 * 1024
OUT_LN_SPLIT = 2
PROJ_SPLIT = 2


def _params(*sem):
    return pltpu.CompilerParams(dimension_semantics=sem, vmem_limit_bytes=VMEM_LIMIT)


def _rope_table_kernel(pos_ref, cst_ref, cr_ref, sr_ref, cm_ref, sa_ref, sb_ref):
    pos = pos_ref[...].astype(F32)
    ang = pos * cst_ref[0:1, :]
    c = jnp.cos(ang)
    s = jnp.sin(ang)
    half = LANES // 2
    quarter = LANES // 4
    c64 = pltpu.roll(c, half, 1)
    s64 = pltpu.roll(s, half, 1)
    cr_ref[...] = c * cst_ref[1:2, :] + c64 * cst_ref[3:4, :]
    sr_ref[...] = s * cst_ref[2:3, :] + s64 * cst_ref[3:4, :]
    cm_ref[...] = c64 * cst_ref[4:5, :] + pltpu.roll(c, half + quarter, 1) * cst_ref[6:7, :]
    sa_ref[...] = s64 * cst_ref[5:6, :]
    sb_ref[...] = pltpu.roll(s, half + quarter, 1) * cst_ref[6:7, :]


def _rope_tables(pos):
    t = pos.shape[0]
    tm = 2048
    half_r = RET_DK // 2
    half_m = MLA_ROPE // 2
    inv_r = ROPE_BASE ** (-jnp.arange(0, RET_DK, 2, dtype=F32) / RET_DK)
    inv_m = ROPE_BASE ** (-jnp.arange(0, MLA_ROPE, 2, dtype=F32) / MLA_ROPE)
    assert half_r == LANES // 2 and half_m == LANES // 4
    lane = jnp.arange(LANES)
    lo64 = (lane < half_r).astype(F32)
    q0 = (lane < half_m).astype(F32)
    q1 = ((lane >= half_m) & (lane < 2 * half_m)).astype(F32)
    cst = jnp.stack([
        jnp.concatenate([inv_r, inv_m, jnp.zeros((LANES - half_r - half_m,), F32)]),
        lo64, -lo64, 1.0 - lo64, q0, -q0, q1, jnp.zeros((LANES,), F32)])
    out = jax.ShapeDtypeStruct((t, LANES), F32)
    tab_spec = pl.BlockSpec((tm, LANES), lambda i: (i, 0))
    return pl.pallas_call(
        _rope_table_kernel,
        grid=(t // tm,),
        in_specs=[pl.BlockSpec((tm, 1), lambda i: (i, 0)),
                  pl.BlockSpec((8, LANES), lambda i: (0, 0))],
        out_specs=[tab_spec] * 5,
        out_shape=[out] * 5,
        compiler_params=_params("parallel"),
        name="rope_tables",
    )(pos, cst)


def _proj_kernel(x_ref, w_ref, c_ref, s_ref, o_ref, xb_ref, *, k_scale):
    j = pl.program_id(1)

    @pl.when(j == 0)
    def _():
        xb_ref[...] = x_ref[...].astype(xb_ref.dtype)

    rotary = j < 2
    scale = jnp.where(j == 1, k_scale, 1.0).astype(F32)
    rows = x_ref.shape[0] // PROJ_SPLIT
    for r0 in range(0, x_ref.shape[0], rows):
        sl = slice(r0, r0 + rows)
        res = jnp.dot(xb_ref[sl, :], w_ref[...], preferred_element_type=F32)
        c = jnp.where(rotary, c_ref[sl, :] * scale, 1.0)
        s = jnp.where(rotary, s_ref[sl, :] * scale, 0.0)
        for h in range(res.shape[1] // RET_DK):
            r = res[:, h * RET_DK:(h + 1) * RET_DK]
            o = r * c + pltpu.roll(r, RET_DK // 2, 1) * s
            o_ref[sl, h * RET_DK:(h + 1) * RET_DK] = o.astype(o_ref.dtype)


def _proj(xf, w, c_tab, s_tab):
    t, d = xf.shape
    tm, tn = 1024, RET_QK
    n = 4 * tn
    assert RET_QK == RET_WIDTH and w.shape[1] >= n
    return pl.pallas_call(
        functools.partial(_proj_kernel, k_scale=RET_DK ** -0.5),
        grid=(t // tm, n // tn),
        in_specs=[pl.BlockSpec((tm, d), lambda i, j: (i, 0)),
                  pl.BlockSpec((d, tn), lambda i, j: (0, j)),
                  pl.BlockSpec((tm, LANES), lambda i, j: (i, 0)),
                  pl.BlockSpec((tm, LANES), lambda i, j: (i, 0))],
        out_specs=[pl.BlockSpec((tm, tn), lambda i, j: (i, j)),
                   pl.BlockSpec((tm, d), lambda i, j: (i, 0))],
        out_shape=[jax.ShapeDtypeStruct((t, n), BF16), jax.ShapeDtypeStruct((t, d), BF16)],
        compiler_params=_params("parallel", "arbitrary"),
        name="proj",
    )(xf, w, c_tab, s_tab)


def _retention_kernel(lg_ref, q_ref, k_ref, v_ref, g_ref, gn_ref, o_ref, d_ref, *, blk):
    lg = lg_ref[pl.program_id(1)]
    row = lax.broadcasted_iota(jnp.int32, (blk, blk), 0)
    col = lax.broadcasted_iota(jnp.int32, (blk, blk), 1)
    dist = jnp.abs(row - col).astype(F32)
    d_ref[...] = jnp.where(col // CHUNK <= row // CHUNK, jnp.exp(lg * dist), 0.0)
    idx = lax.broadcasted_iota(jnp.int32, (blk, 1), 0).astype(F32)
    q_dec = jnp.exp(lg * (idx + 1.0))
    k_dec = jnp.exp(lg * (blk - 1.0 - idx))
    blk_dec = jnp.exp(jnp.full((1, 1), blk, F32) * lg)
    gn = gn_ref[...]
    nt = (((1,), (1,)), ((), ()))
    tn = (((0,), (0,)), ((), ()))

    def body(n, state):
        sl = pl.ds(pl.multiple_of(n * blk, blk), blk)
        q = q_ref[sl, :]
        k = k_ref[sl, :]
        v = v_ref[sl, :]
        s = lax.dot_general(q, k, nt, preferred_element_type=F32) * d_ref[...]
        intra = jnp.dot(s.astype(BF16), v, preferred_element_type=F32)
        cross = jnp.dot(q, state.astype(BF16), preferred_element_type=F32) * q_dec
        kd = (k.astype(F32) * k_dec).astype(BF16)
        new_state = blk_dec * state + lax.dot_general(kd, v, tn, preferred_element_type=F32)
        ret = intra + cross
        mu = jnp.mean(ret, axis=-1, keepdims=True)
        dev = ret - mu
        var = jnp.mean(dev * dev, axis=-1, keepdims=True)
        y = dev * lax.rsqrt(var + EPS) * gn
        g = g_ref[sl, :].astype(F32)
        gate = g / (1.0 + jnp.exp(-g))
        o_ref[sl, :] = (gate * y).astype(o_ref.dtype)
        return new_state

    lax.fori_loop(0, q_ref.shape[0] // blk, body, jnp.zeros((RET_DK, RET_DV), F32), unroll=True)


def _retention(qkvg, gn_g, log_g, batch, seq):
    t = qkvg.shape[0]
    blk = 256
    h = RET_HEADS
    return pl.pallas_call(
        functools.partial(_retention_kernel, blk=blk),
        grid=(batch, h),
        in_specs=[pl.BlockSpec(memory_space=pltpu.SMEM),
                  pl.BlockSpec((seq, RET_DK), lambda b, i: (b, i)),
                  pl.BlockSpec((seq, RET_DK), lambda b, i: (b, h + i)),
                  pl.BlockSpec((seq, RET_DV), lambda b, i: (b, 2 * h + i)),
                  pl.BlockSpec((seq, RET_DV), lambda b, i: (b, 3 * h + i)),
                  pl.BlockSpec((1, RET_DV), lambda b, i: (0, i))],
        out_specs=pl.BlockSpec((seq, RET_DV), lambda b, i: (b, i)),
        out_shape=jax.ShapeDtypeStruct((t, RET_WIDTH), BF16),
        scratch_shapes=[pltpu.VMEM((blk, blk), F32)],
        compiler_params=_params("parallel", "arbitrary"),
        name="retention",
    )(log_g, qkvg, qkvg, qkvg, qkvg, gn_g)


def _rms(v, g):
    return v * lax.rsqrt(jnp.mean(v * v, axis=-1, keepdims=True) + EPS) * g


def _rope64(r, cm, sa, sb):
    half = MLA_ROPE // 2
    return r * cm + pltpu.roll(r, LANES - half, 1) * sa + pltpu.roll(r, half, 1) * sb


def _mla_prep_kernel(x_ref, wcq_ref, wckv_ref, wkr_ref, qg_ref, kvg_ref, wuq_ref, wuk_ref, wuv_ref,
                     cm_ref, sa_ref, sb_ref, q_ref, k_ref, v_ref):
    x = x_ref[...]
    cm = cm_ref[...]
    sa = sa_ref[...]
    sb = sb_ref[...]
    cq = jnp.dot(x, wcq_ref[...], preferred_element_type=F32)
    cqn = _rms(cq, qg_ref[...]).astype(BF16)
    q = jnp.dot(cqn, wuq_ref[...], preferred_element_type=F32) * MLA_Q_SCALE
    rope0 = MLA_HEADS * MLA_NOPE
    for h in range(MLA_HEADS):
        lo = h * MLA_QK_PAD
        q_ref[:, lo:lo + MLA_NOPE] = q[:, h * MLA_NOPE:(h + 1) * MLA_NOPE].astype(q_ref.dtype)
        pair = q[:, rope0 + (h // 2) * LANES:rope0 + (h // 2 + 1) * LANES]
        if h % 2:
            pair = pltpu.roll(pair, LANES // 2, 1)
        q_ref[:, lo + MLA_NOPE:lo + MLA_QK_PAD] = _rope64(pair, cm, sa, sb).astype(q_ref.dtype)
    ckv = jnp.dot(x, wckv_ref[...], preferred_element_type=F32)
    ckvn = _rms(ckv, kvg_ref[...]).astype(BF16)
    kr = jnp.dot(x, wkr_ref[...], preferred_element_type=F32)
    k_rope = _rope64(kr, cm, sa, sb).astype(k_ref.dtype)
    kn = jnp.dot(ckvn, wuk_ref[...], preferred_element_type=F32)
    for h in range(MLA_HEADS):
        lo = h * MLA_QK_PAD
        k_ref[:, lo:lo + MLA_NOPE] = kn[:, h * MLA_NOPE:(h + 1) * MLA_NOPE].astype(k_ref.dtype)
        k_ref[:, lo + MLA_NOPE:lo + MLA_QK_PAD] = k_rope
    v_ref[...] = jnp.dot(ckvn, wuv_ref[...], preferred_element_type=F32).astype(v_ref.dtype)


def _mla_prep(xb, wcq, wckv, wkr, qg, kvg, wuq, wuk, wuv, cm, sa, sb):
    t, d = xb.shape
    tm = 1024
    row = lambda i: (i, 0)
    whole = lambda i: (0, 0)
    full = lambda a: pl.BlockSpec(a.shape, whole, pipeline_mode=pl.Buffered(1))
    qk_w = MLA_HEADS * MLA_QK_PAD
    return pl.pallas_call(
        _mla_prep_kernel,
        grid=(t // tm,),
        in_specs=[pl.BlockSpec((tm, d), row), full(wcq), full(wckv), full(wkr), full(qg), full(kvg),
                  full(wuq), full(wuk), full(wuv),
                  pl.BlockSpec((tm, LANES), row), pl.BlockSpec((tm, LANES), row),
                  pl.BlockSpec((tm, LANES), row)],
        out_specs=[pl.BlockSpec((tm, qk_w), row), pl.BlockSpec((tm, qk_w), row),
                   pl.BlockSpec((tm, MLA_WIDTH), row)],
        out_shape=[jax.ShapeDtypeStruct((t, qk_w), BF16), jax.ShapeDtypeStruct((t, qk_w), BF16),
                   jax.ShapeDtypeStruct((t, MLA_WIDTH), BF16)],
        compiler_params=_params("parallel"),
        name="mla_prep",
    )(xb, wcq, wckv, wkr, qg, kvg, wuq, wuk, wuv, cm, sa, sb)


def _mla_attn_kernel(q_ref, k_ref, v_ref, o_ref, sa_ref, sb_ref, pa_ref, pb_ref, acc_ref, qt_ref, *,
                     qblk, kblk):
    tn = (((0,), (0,)), ((), ()))

    def keys(j):
        return pl.ds(pl.multiple_of(j * kblk, kblk), kblk)

    everything = slice(None)

    def scores(s_ref, j, cols=everything):
        s_ref[:, cols] = jnp.dot(k_ref[keys(j), :], qt_ref[:, cols], preferred_element_type=F32)

    def probs(s_ref, p_ref, stats, cols=everything, masked=False):
        m, l = stats
        st = s_ref[:, cols]
        if masked:
            key = lax.broadcasted_iota(jnp.int32, st.shape, 0)
            qry = lax.broadcasted_iota(jnp.int32, st.shape, 1)
            st = jnp.where(key // CHUNK <= qry // CHUNK, st, -jnp.inf)
        m_new = jnp.maximum(m, jnp.max(st, axis=0, keepdims=True))
        alpha = jnp.exp2(m - m_new)
        p = jnp.exp2(st - m_new)
        p_ref[:, cols] = p.astype(p_ref.dtype)
        return (m_new, alpha * l + jnp.sum(p, axis=0, keepdims=True)), alpha

    def accum(p_ref, j, alpha, cols=everything):
        pv = lax.dot_general(v_ref[keys(j), :], p_ref[:, cols], tn, preferred_element_type=F32)
        acc_ref[:, cols] = alpha * acc_ref[:, cols] + pv

    def body(jj, carry):
        stats, alpha = carry
        j = 2 * jj
        scores(sb_ref, j + 1)
        accum(pb_ref, jnp.maximum(j - 1, 0), alpha)
        stats, alpha = probs(sa_ref, pa_ref, stats)
        scores(sa_ref, j + 2)
        accum(pa_ref, j, alpha)
        stats, alpha = probs(sb_ref, pb_ref, stats)
        return stats, alpha

    nq = q_ref.shape[0] // qblk

    def query_rows(qi):
        return pl.ds(pl.multiple_of(qi * qblk, qblk), qblk)

    def start_query_block(qi):
        qt_ref[...] = q_ref[query_rows(qi), :].T
        scores(sa_ref, 0)

    def query_block(qi, _):
        acc_ref[...] = jnp.zeros_like(acc_ref)
        pb_ref[...] = jnp.zeros_like(pb_ref)
        stats = (jnp.full((1, qblk), -jnp.inf, F32), jnp.zeros((1, qblk), F32))
        nfull = qi * (qblk // kblk)
        stats, alpha = lax.fori_loop(0, nfull // 2, body, (stats, jnp.ones((1, qblk), F32)))
        late = slice(kblk, qblk)
        scores(sb_ref, nfull + 1, late)
        accum(pb_ref, jnp.maximum(nfull - 1, 0), alpha)
        (m, l), alpha = probs(sa_ref, pa_ref, stats, masked=True)
        start_query_block(jnp.minimum(qi + 1, nq - 1))
        accum(pa_ref, nfull, alpha)
        (_, l_late), alpha = probs(sb_ref, pb_ref, (m[:, late], l[:, late]), late, masked=True)
        accum(pb_ref, nfull + 1, alpha, late)
        l = jnp.concatenate([l[:, :kblk], l_late], axis=1)
        o_ref[:, query_rows(qi)] = (acc_ref[...] / l).astype(o_ref.dtype)
        return 0

    start_query_block(0)
    lax.fori_loop(0, nq, query_block, 0)


def _mla_attn(q, k, v, batch, seq):
    t = q.shape[0]
    qblk = 1024
    kblk = qblk // 2
    per_head = lambda b, h: (b, h)
    return pl.pallas_call(
        functools.partial(_mla_attn_kernel, qblk=qblk, kblk=kblk),
        grid=(batch, MLA_HEADS),
        in_specs=[pl.BlockSpec((seq, MLA_QK_PAD), per_head),
                  pl.BlockSpec((seq, MLA_QK_PAD), per_head),
                  pl.BlockSpec((seq, MLA_DV), per_head)],
        out_specs=pl.BlockSpec((MLA_DV, seq), lambda b, h: (h, b)),
        out_shape=jax.ShapeDtypeStruct((MLA_WIDTH, t), BF16),
        scratch_shapes=[pltpu.VMEM((kblk, qblk), F32), pltpu.VMEM((kblk, qblk), F32),
                        pltpu.VMEM((kblk, qblk), BF16), pltpu.VMEM((kblk, qblk), BF16),
                        pltpu.VMEM((MLA_DV, qblk), F32), pltpu.VMEM((MLA_QK_PAD, qblk), BF16)],
        compiler_params=_params("parallel", "arbitrary"),
        name="mla_attn",
    )(q, k, v)


def _layer_norm(y, g, b):
    mu = jnp.mean(y, axis=-1, keepdims=True)
    dev = y - mu
    var = jnp.mean(dev * dev, axis=-1, keepdims=True)
    return dev * lax.rsqrt(var + EPS) * g + b


def _out_ln_kernel(ret_ref, mlat_ref, x_ref, w_ref, g_ref, b_ref, o_ref, ob_ref, *, alpha):
    kr = ret_ref.shape[1]
    rows = x_ref.shape[0] // OUT_LN_SPLIT
    for r in range(OUT_LN_SPLIT):
        sl = slice(r * rows, (r + 1) * rows)
        mix = jnp.dot(ret_ref[sl, :], w_ref[0:kr, :], preferred_element_type=F32)
        mix = mix + lax.dot_general(mlat_ref[:, sl], w_ref[kr:, :], (((0,), (0,)), ((), ())),
                                    preferred_element_type=F32)
        y = alpha * x_ref[sl, :] + mix
        out = _layer_norm(y, g_ref[...], b_ref[...])
        o_ref[sl, :] = out
        ob_ref[sl, :] = out.astype(ob_ref.dtype)


def _out_ln(ret, mlat, xf, w, g, b, alpha):
    t, d = xf.shape
    tm = 512
    row = lambda i: (i, 0)
    whole = lambda i: (0, 0)
    return pl.pallas_call(
        functools.partial(_out_ln_kernel, alpha=alpha),
        grid=(t // tm,),
        in_specs=[pl.BlockSpec((tm, ret.shape[1]), row),
                  pl.BlockSpec((mlat.shape[0], tm), lambda i: (0, i)),
                  pl.BlockSpec((tm, d), row), pl.BlockSpec(w.shape, whole),
                  pl.BlockSpec((1, d), whole), pl.BlockSpec((1, d), whole)],
        out_specs=[pl.BlockSpec((tm, d), row), pl.BlockSpec((tm, d), row)],
        out_shape=[jax.ShapeDtypeStruct((t, d), F32), jax.ShapeDtypeStruct((t, d), BF16)],
        compiler_params=_params("parallel"),
        name="out_ln",
    )(ret, mlat, xf, w, g, b)


def _ffn_ln_kernel(x_ref, xb_ref, wu_ref, wd_ref, g_ref, b_ref, o_ref, acc_ref, y_ref, *, alpha, nf):
    t = pl.program_id(0)

    def mlp():
        u = jnp.dot(xb_ref[...], wu_ref[...], preferred_element_type=F32)
        a = jnp.square(jnp.maximum(u, 0.0)).astype(BF16)
        return jnp.dot(a, wd_ref[...], preferred_element_type=F32)

    @pl.when(t == 0)
    def _():
        acc_ref[...] = jnp.zeros_like(acc_ref)

    @pl.when(t % nf == 0)
    def _():
        y_ref[...] = alpha * x_ref[...] + acc_ref[...]
        o_ref[...] = _layer_norm(y_ref[...], g_ref[...], b_ref[...])
        acc_ref[...] = mlp()

    @pl.when(t % nf != 0)
    def _():
        acc_ref[...] += mlp()


def _ffn_ln(x1, x1b, wu, wd, g, b, alpha):
    t, d = x1.shape
    dff = wu.shape[1]
    tm, tf = 512, 1024
    nt, nf = t // tm, dff // tf
    last = nt * nf - 1
    rows_now = lambda s: (jnp.minimum(s, last) // nf, 0)
    rows_late = lambda s: (jnp.maximum(s - 1, 0) // nf, 0)
    hidden = lambda s: jnp.minimum(s, last) % nf
    return pl.pallas_call(
        functools.partial(_ffn_ln_kernel, alpha=alpha, nf=nf),
        grid=(nt * nf + 1,),
        in_specs=[pl.BlockSpec((tm, d), rows_late),
                  pl.BlockSpec((tm, d), rows_now),
                  pl.BlockSpec((d, tf), lambda s: (0, hidden(s))),
                  pl.BlockSpec((tf, d), lambda s: (hidden(s), 0)),
                  pl.BlockSpec((1, d), lambda s: (0, 0)),
                  pl.BlockSpec((1, d), lambda s: (0, 0))],
        out_specs=pl.BlockSpec((tm, d), rows_late),
        out_shape=jax.ShapeDtypeStruct((t, d), F32),
        scratch_shapes=[pltpu.VMEM((tm, d), F32), pltpu.VMEM((tm, d), F32)],
        compiler_params=_params("arbitrary"),
        name="ffn_ln",
    )(x1, x1b, wu, wd, g, b)


def kernel(x, positions, w_in, q_norm_g, w_uq, kv_norm_g, w_uk, w_uv, ret_gn_g, w_out, ln1_g, ln1_b,
           w_up, w_down, ln2_g, ln2_b):
    batch, seq, d = x.shape
    depth = w_in.shape[0]
    t = batch * seq
    alpha = (2.0 * depth) ** 0.25
    xf = x.reshape(t, d)
    c_ret, s_ret, c_mla, sa_mla, sb_mla = _rope_tables(positions.reshape(t, 1))
    log_g = jnp.log1p(-jnp.exp2(-5.0 - jnp.arange(RET_HEADS, dtype=F32)))
    o_cq = 2 * RET_QK + 2 * RET_WIDTH
    o_ckv = o_cq + MLA_Q_RANK
    o_kr = o_ckv + MLA_KV_RANK
    for l in range(depth):
        w = w_in[l].astype(BF16)
        w_cq = w[:, o_cq:o_ckv]
        w_ckv = w[:, o_ckv:o_kr]
        w_kr = jnp.pad(w[:, o_kr:], ((0, 0), (0, LANES - MLA_ROPE)))
        wuq = w_uq[l].reshape(MLA_Q_RANK, MLA_HEADS, MLA_NOPE + MLA_ROPE)
        wuq = jnp.concatenate([wuq[:, :, :MLA_NOPE].reshape(MLA_Q_RANK, -1),
                               wuq[:, :, MLA_NOPE:].reshape(MLA_Q_RANK, -1)], axis=1).astype(BF16)

        qkvg, xb = _proj(xf, w, c_ret, s_ret)
        ret = _retention(qkvg, ret_gn_g[l].reshape(1, RET_WIDTH), log_g, batch, seq)
        q, k, v = _mla_prep(xb, w_cq, w_ckv, w_kr, q_norm_g[l].reshape(1, -1),
                            kv_norm_g[l].reshape(1, -1), wuq, w_uk[l].astype(BF16),
                            w_uv[l].astype(BF16), c_mla, sa_mla, sb_mla)
        mla = _mla_attn(q, k, v, batch, seq)
        x1, x1b = _out_ln(ret, mla, xf, w_out[l].astype(BF16), ln1_g[l].reshape(1, d),
                     ln1_b[l].reshape(1, d), alpha)
        xf = _ffn_ln(x1, x1b, w_up[l].astype(BF16), w_down[l].astype(BF16), ln2_g[l].reshape(1, d),
                     ln2_b[l].reshape(1, d), alpha)
    return xf.reshape(batch, seq, d)
```

```python
import functools

import jax
import jax.numpy as jnp
from jax import lax
from jax.experimental import pallas as pl
from jax.experimental.pallas import tpu as pltpu

F32 = jnp.float32
BF16 = jnp.bfloat16

LANES = 128
CHUNK = 64
RET_HEADS = 8
RET_DK = 128
RET_DV = 128
RET_QK = RET_HEADS * RET_DK
RET_WIDTH = RET_HEADS * RET_DV
MLA_HEADS = 8
MLA_NOPE = 128
MLA_ROPE = 64
MLA_DV = 128
MLA_Q_RANK = 768
MLA_KV_RANK = 512
MLA_QK_PAD = 256
MLA_WIDTH = MLA_HEADS * MLA_DV
MLA_Q_SCALE = (MLA_NOPE + MLA_ROPE) ** -0.5 * 1.4426950408889634
ROPE_BASE = 10000.0
EPS = 1e-5

VMEM_LIMIT = 56 * 1024 * 1024
OUT_LN_SPLIT = 2
PROJ_SPLIT = 2


def _params(*sem):
    return pltpu.CompilerParams(dimension_semantics=sem, vmem_limit_bytes=VMEM_LIMIT)


def _rope_table_kernel(pos_ref, cst_ref, cr_ref, sr_ref, cm_ref, sa_ref, sb_ref):
    pos = pos_ref[...].astype(F32)
    ang = pos * cst_ref[0:1, :]
    c = jnp.cos(ang)
    s = jnp.sin(ang)
    half = LANES // 2
    quarter = LANES // 4
    c64 = pltpu.roll(c, half, 1)
    s64 = pltpu.roll(s, half, 1)
    cr_ref[...] = c * cst_ref[1:2, :] + c64 * cst_ref[3:4, :]
    sr_ref[...] = s * cst_ref[2:3, :] + s64 * cst_ref[3:4, :]
    cm_ref[...] = c64 * cst_ref[4:5, :] + pltpu.roll(c, half + quarter, 1) * cst_ref[6:7, :]
    sa_ref[...] = s64 * cst_ref[5:6, :]
    sb_ref[...] = pltpu.roll(s, half + quarter, 1) * cst_ref[6:7, :]


def _rope_tables(pos):
    t = pos.shape[0]
    tm = 2048
    half_r = RET_DK // 2
    half_m = MLA_ROPE // 2
    inv_r = ROPE_BASE ** (-jnp.arange(0, RET_DK, 2, dtype=F32) / RET_DK)
    inv_m = ROPE_BASE ** (-jnp.arange(0, MLA_ROPE, 2, dtype=F32) / MLA_ROPE)
    assert half_r == LANES // 2 and half_m == LANES // 4
    lane = jnp.arange(LANES)
    lo64 = (lane < half_r).astype(F32)
    q0 = (lane < half_m).astype(F32)
    q1 = ((lane >= half_m) & (lane < 2 * half_m)).astype(F32)
    cst = jnp.stack([
        jnp.concatenate([inv_r, inv_m, jnp.zeros((LANES - half_r - half_m,), F32)]),
        lo64, -lo64, 1.0 - lo64, q0, -q0, q1, jnp.zeros((LANES,), F32)])
    out = jax.ShapeDtypeStruct((t, LANES), F32)
    tab_spec = pl.BlockSpec((tm, LANES), lambda i: (i, 0))
    return pl.pallas_call(
        _rope_table_kernel,
        grid=(t // tm,),
        in_specs=[pl.BlockSpec((tm, 1), lambda i: (i, 0)),
                  pl.BlockSpec((8, LANES), lambda i: (0, 0))],
        out_specs=[tab_spec] * 5,
        out_shape=[out] * 5,
        compiler_params=_params("parallel"),
        name="rope_tables",
    )(pos, cst)


def _proj_kernel(x_ref, w_ref, c_ref, s_ref, o_ref, wb_ref, *, k_scale):
    j = pl.program_id(0)

    @pl.when(pl.program_id(1) == 0)
    def _():
        wb_ref[...] = w_ref[...].astype(wb_ref.dtype)

    rotary = j < 2
    scale = jnp.where(j == 1, k_scale, 1.0).astype(F32)
    rows = x_ref.shape[0] // PROJ_SPLIT
    for r0 in range(0, x_ref.shape[0], rows):
        sl = slice(r0, r0 + rows)
        xb = x_ref[sl, :].astype(wb_ref.dtype)
        res = jnp.dot(xb, wb_ref[...], preferred_element_type=F32)
        c = jnp.where(rotary, c_ref[sl, :] * scale, 1.0)
        s = jnp.where(rotary, s_ref[sl, :] * scale, 0.0)
        for h in range(res.shape[1] // RET_DK):
            r = res[:, h * RET_DK:(h + 1) * RET_DK]
            o = r * c + pltpu.roll(r, RET_DK // 2, 1) * s
            o_ref[sl, h * RET_DK:(h + 1) * RET_DK] = o.astype(o_ref.dtype)


def _proj(xf, w, c_tab, s_tab):
    t, d = xf.shape
    tm, tn = 1024, RET_QK
    n = 4 * tn
    assert RET_QK == RET_WIDTH and w.shape[1] >= n
    rows = lambda j, i: (i, 0)
    return pl.pallas_call(
        functools.partial(_proj_kernel, k_scale=RET_DK ** -0.5),
        grid=(n // tn, t // tm),
        in_specs=[pl.BlockSpec((tm, d), rows),
                  pl.BlockSpec((d, tn), lambda j, i: (0, j)),
                  pl.BlockSpec((tm, LANES), rows),
                  pl.BlockSpec((tm, LANES), rows)],
        out_specs=pl.BlockSpec((tm, tn), lambda j, i: (i, j)),
        out_shape=jax.ShapeDtypeStruct((t, n), BF16),
        scratch_shapes=[pltpu.VMEM((d, tn), BF16)],
        compiler_params=_params("arbitrary", "arbitrary"),
        name="proj",
    )(xf, w, c_tab, s_tab)


def _retention_kernel(lg_ref, q_ref, k_ref, v_ref, g_ref, gn_ref, o_ref, d_ref, *, blk):
    lg = lg_ref[pl.program_id(1)]
    row = lax.broadcasted_iota(jnp.int32, (blk, blk), 0)
    col = lax.broadcasted_iota(jnp.int32, (blk, blk), 1)
    dist = jnp.abs(row - col).astype(F32)
    d_ref[...] = jnp.where(col // CHUNK <= row // CHUNK, jnp.exp(lg * dist), 0.0)
    idx = lax.broadcasted_iota(jnp.int32, (blk, 1), 0).astype(F32)
    q_dec = jnp.exp(lg * (idx + 1.0))
    k_dec = jnp.exp(lg * (blk - 1.0 - idx))
    blk_dec = jnp.exp(jnp.full((1, 1), blk, F32) * lg)
    gn = gn_ref[...]
    nt = (((1,), (1,)), ((), ()))
    tn = (((0,), (0,)), ((), ()))

    def body(n, state):
        sl = pl.ds(pl.multiple_of(n * blk, blk), blk)
        q = q_ref[sl, :]
        k = k_ref[sl, :]
        v = v_ref[sl, :]
        s = lax.dot_general(q, k, nt, preferred_element_type=F32) * d_ref[...]
        intra = jnp.dot(s.astype(BF16), v, preferred_element_type=F32)
        cross = jnp.dot(q, state.astype(BF16), preferred_element_type=F32) * q_dec
        kd = (k.astype(F32) * k_dec).astype(BF16)
        new_state = blk_dec * state + lax.dot_general(kd, v, tn, preferred_element_type=F32)
        ret = intra + cross
        mu = jnp.mean(ret, axis=-1, keepdims=True)
        dev = ret - mu
        var = jnp.mean(dev * dev, axis=-1, keepdims=True)
        y = dev * lax.rsqrt(var + EPS) * gn
        g = g_ref[sl, :].astype(F32)
        gate = g / (1.0 + jnp.exp(-g))
        o_ref[sl, :] = (gate * y).astype(o_ref.dtype)
        return new_state

    lax.fori_loop(0, q_ref.shape[0] // blk, body, jnp.zeros((RET_DK, RET_DV), F32), unroll=True)


def _retention(qkvg, gn_g, log_g, batch, seq):
    t = qkvg.shape[0]
    blk = 256
    h = RET_HEADS
    return pl.pallas_call(
        functools.partial(_retention_kernel, blk=blk),
        grid=(batch, h),
        in_specs=[pl.BlockSpec(memory_space=pltpu.SMEM),
                  pl.BlockSpec((seq, RET_DK), lambda b, i: (b, i)),
                  pl.BlockSpec((seq, RET_DK), lambda b, i: (b, h + i)),
                  pl.BlockSpec((seq, RET_DV), lambda b, i: (b, 2 * h + i)),
                  pl.BlockSpec((seq, RET_DV), lambda b, i: (b, 3 * h + i)),
                  pl.BlockSpec((1, RET_DV), lambda b, i: (0, i))],
        out_specs=pl.BlockSpec((seq, RET_DV), lambda b, i: (b, i)),
        out_shape=jax.ShapeDtypeStruct((t, RET_WIDTH), BF16),
        scratch_shapes=[pltpu.VMEM((blk, blk), F32)],
        compiler_params=_params("parallel", "arbitrary"),
        name="retention",
    )(log_g, qkvg, qkvg, qkvg, qkvg, gn_g)


def _rms(v, g):
    return v * lax.rsqrt(jnp.mean(v * v, axis=-1, keepdims=True) + EPS) * g


def _rope64(r, cm, sa, sb):
    half = MLA_ROPE // 2
    return r * cm + pltpu.roll(r, LANES - half, 1) * sa + pltpu.roll(r, half, 1) * sb


def _mla_prep_kernel(x_ref, wcq_ref, wckv_ref, wkr_ref, qg_ref, kvg_ref, wuq_ref, wuk_ref, wuv_ref,
                     cm_ref, sa_ref, sb_ref, q_ref, k_ref, v_ref):
    x = x_ref[...].astype(BF16)
    cm = cm_ref[...]
    sa = sa_ref[...]
    sb = sb_ref[...]
    cq = jnp.dot(x, wcq_ref[...], preferred_element_type=F32)
    cqn = _rms(cq, qg_ref[...]).astype(BF16)
    q = jnp.dot(cqn, wuq_ref[...], preferred_element_type=F32) * MLA_Q_SCALE
    rope0 = MLA_HEADS * MLA_NOPE
    for h in range(MLA_HEADS):
        lo = h * MLA_QK_PAD
        q_ref[:, lo:lo + MLA_NOPE] = q[:, h * MLA_NOPE:(h + 1) * MLA_NOPE].astype(q_ref.dtype)
        pair = q[:, rope0 + (h // 2) * LANES:rope0 + (h // 2 + 1) * LANES]
        if h % 2:
            pair = pltpu.roll(pair, LANES // 2, 1)
        q_ref[:, lo + MLA_NOPE:lo + MLA_QK_PAD] = _rope64(pair, cm, sa, sb).astype(q_ref.dtype)
    ckv = jnp.dot(x, wckv_ref[...], preferred_element_type=F32)
    ckvn = _rms(ckv, kvg_ref[...]).astype(BF16)
    kr = jnp.dot(x, wkr_ref[...], preferred_element_type=F32)
    k_rope = _rope64(kr, cm, sa, sb).astype(k_ref.dtype)
    kn = jnp.dot(ckvn, wuk_ref[...], preferred_element_type=F32)
    for h in range(MLA_HEADS):
        lo = h * MLA_QK_PAD
        k_ref[:, lo:lo + MLA_NOPE] = kn[:, h * MLA_NOPE:(h + 1) * MLA_NOPE].astype(k_ref.dtype)
        k_ref[:, lo + MLA_NOPE:lo + MLA_QK_PAD] = k_rope
    v_ref[...] = jnp.dot(ckvn, wuv_ref[...], preferred_element_type=F32).astype(v_ref.dtype)


def _mla_prep(xf, wcq, wckv, wkr, qg, kvg, wuq, wuk, wuv, cm, sa, sb):
    t, d = xf.shape
    tm = 512
    row = lambda i: (i, 0)
    whole = lambda i: (0, 0)
    full = lambda a: pl.BlockSpec(a.shape, whole, pipeline_mode=pl.Buffered(1))
    qk_w = MLA_HEADS * MLA_QK_PAD
    return pl.pallas_call(
        _mla_prep_kernel,
        grid=(t // tm,),
        in_specs=[pl.BlockSpec((tm, d), row), full(wcq), full(wckv), full(wkr), full(qg), full(kvg),
                  full(wuq), full(wuk), full(wuv),
                  pl.BlockSpec((tm, LANES), row), pl.BlockSpec((tm, LANES), row),
                  pl.BlockSpec((tm, LANES), row)],
        out_specs=[pl.BlockSpec((tm, qk_w), row), pl.BlockSpec((tm, qk_w), row),
                   pl.BlockSpec((tm, MLA_WIDTH), row)],
        out_shape=[jax.ShapeDtypeStruct((t, qk_w), BF16), jax.ShapeDtypeStruct((t, qk_w), BF16),
                   jax.ShapeDtypeStruct((t, MLA_WIDTH), BF16)],
        compiler_params=_params("parallel"),
        name="mla_prep",
    )(xf, wcq, wckv, wkr, qg, kvg, wuq, wuk, wuv, cm, sa, sb)


def _mla_attn_kernel(q_ref, k_ref, v_ref, o_ref, sa_ref, sb_ref, pa_ref, pb_ref, acc_ref, qt_ref, *,
                     qblk, kblk):
    tn = (((0,), (0,)), ((), ()))

    def keys(j):
        return pl.ds(pl.multiple_of(j * kblk, kblk), kblk)

    everything = slice(None)

    def scores(s_ref, j, cols=everything):
        s_ref[:, cols] = jnp.dot(k_ref[keys(j), :], qt_ref[:, cols], preferred_element_type=F32)

    def probs(s_ref, p_ref, stats, cols=everything, masked=False):
        m, l = stats
        st = s_ref[:, cols]
        if masked:
            key = lax.broadcasted_iota(jnp.int32, st.shape, 0)
            qry = lax.broadcasted_iota(jnp.int32, st.shape, 1)
            st = jnp.where(key // CHUNK <= qry // CHUNK, st, -jnp.inf)
        m_new = jnp.maximum(m, jnp.max(st, axis=0, keepdims=True))
        alpha = jnp.exp2(m - m_new)
        p = jnp.exp2(st - m_new)
        p_ref[:, cols] = p.astype(p_ref.dtype)
        return (m_new, alpha * l + jnp.sum(p, axis=0, keepdims=True)), alpha

    def accum(p_ref, j, alpha, cols=everything):
        pv = lax.dot_general(v_ref[keys(j), :], p_ref[:, cols], tn, preferred_element_type=F32)
        acc_ref[:, cols] = alpha * acc_ref[:, cols] + pv

    def body(jj, carry):
        stats, alpha = carry
        j = 2 * jj
        scores(sb_ref, j + 1)
        accum(pb_ref, jnp.maximum(j - 1, 0), alpha)
        stats, alpha = probs(sa_ref, pa_ref, stats)
        scores(sa_ref, j + 2)
        accum(pa_ref, j, alpha)
        stats, alpha = probs(sb_ref, pb_ref, stats)
        return stats, alpha

    nq = q_ref.shape[0] // qblk

    def query_rows(qi):
        return pl.ds(pl.multiple_of(qi * qblk, qblk), qblk)

    def start_query_block(qi):
        qt_ref[...] = q_ref[query_rows(qi), :].T
        scores(sa_ref, 0)

    def query_block(qi, _):
        acc_ref[...] = jnp.zeros_like(acc_ref)
        pb_ref[...] = jnp.zeros_like(pb_ref)
        stats = (jnp.full((1, qblk), -jnp.inf, F32), jnp.zeros((1, qblk), F32))
        nfull = qi * (qblk // kblk)
        stats, alpha = lax.fori_loop(0, nfull // 2, body, (stats, jnp.ones((1, qblk), F32)))
        late = slice(kblk, qblk)
        scores(sb_ref, nfull + 1, late)
        accum(pb_ref, jnp.maximum(nfull - 1, 0), alpha)
        (m, l), alpha = probs(sa_ref, pa_ref, stats, masked=True)
        start_query_block(jnp.minimum(qi + 1, nq - 1))
        accum(pa_ref, nfull, alpha)
        (_, l_late), alpha = probs(sb_ref, pb_ref, (m[:, late], l[:, late]), late, masked=True)
        accum(pb_ref, nfull + 1, alpha, late)
        l = jnp.concatenate([l[:, :kblk], l_late], axis=1)
        o_ref[:, query_rows(qi)] = (acc_ref[...] / l).astype(o_ref.dtype)
        return 0

    start_query_block(0)
    lax.fori_loop(0, nq, query_block, 0)


def _mla_attn(q, k, v, batch, seq):
    t = q.shape[0]
    qblk = 1024
    kblk = qblk // 2
    per_head = lambda b, h: (b, h)
    return pl.pallas_call(
        functools.partial(_mla_attn_kernel, qblk=qblk, kblk=kblk),
        grid=(batch, MLA_HEADS),
        in_specs=[pl.BlockSpec((seq, MLA_QK_PAD), per_head),
                  pl.BlockSpec((seq, MLA_QK_PAD), per_head),
                  pl.BlockSpec((seq, MLA_DV), per_head)],
        out_specs=pl.BlockSpec((MLA_DV, seq), lambda b, h: (h, b)),
        out_shape=jax.ShapeDtypeStruct((MLA_WIDTH, t), BF16),
        scratch_shapes=[pltpu.VMEM((kblk, qblk), F32), pltpu.VMEM((kblk, qblk), F32),
                        pltpu.VMEM((kblk, qblk), BF16), pltpu.VMEM((kblk, qblk), BF16),
                        pltpu.VMEM((MLA_DV, qblk), F32), pltpu.VMEM((MLA_QK_PAD, qblk), BF16)],
        compiler_params=_params("parallel", "arbitrary"),
        name="mla_attn",
    )(q, k, v)


def _layer_norm(y, g, b):
    mu = jnp.mean(y, axis=-1, keepdims=True)
    dev = y - mu
    var = jnp.mean(dev * dev, axis=-1, keepdims=True)
    return dev * lax.rsqrt(var + EPS) * g + b


def _out_ln_kernel(ret_ref, mlat_ref, x_ref, w_ref, g_ref, b_ref, o_ref, ob_ref, wb_ref, *, alpha):
    @pl.when(pl.program_id(0) == 0)
    def _():
        wb_ref[...] = w_ref[...].astype(wb_ref.dtype)

    kr = ret_ref.shape[1]
    rows = x_ref.shape[0] // OUT_LN_SPLIT
    for r in range(OUT_LN_SPLIT):
        sl = slice(r * rows, (r + 1) * rows)
        mix = jnp.dot(ret_ref[sl, :], wb_ref[0:kr, :], preferred_element_type=F32)
        mix = mix + lax.dot_general(mlat_ref[:, sl], wb_ref[kr:, :], (((0,), (0,)), ((), ())),
                                    preferred_element_type=F32)
        y = alpha * x_ref[sl, :] + mix
        out = _layer_norm(y, g_ref[...], b_ref[...])
        o_ref[sl, :] = out
        ob_ref[sl, :] = out.astype(ob_ref.dtype)


def _out_ln(ret, mlat, xf, w, g, b, alpha):
    t, d = xf.shape
    tm = 512
    row = lambda i: (i, 0)
    whole = lambda i: (0, 0)
    return pl.pallas_call(
        functools.partial(_out_ln_kernel, alpha=alpha),
        grid=(t // tm,),
        in_specs=[pl.BlockSpec((tm, ret.shape[1]), row),
                  pl.BlockSpec((mlat.shape[0], tm), lambda i: (0, i)),
                  pl.BlockSpec((tm, d), row),
                  pl.BlockSpec(w.shape, whole, pipeline_mode=pl.Buffered(1)),
                  pl.BlockSpec((1, d), whole), pl.BlockSpec((1, d), whole)],
        out_specs=[pl.BlockSpec((tm, d), row), pl.BlockSpec((tm, d), row)],
        out_shape=[jax.ShapeDtypeStruct((t, d), F32), jax.ShapeDtypeStruct((t, d), BF16)],
        scratch_shapes=[pltpu.VMEM(w.shape, BF16)],
        compiler_params=_params("arbitrary"),
        name="out_ln",
    )(ret, mlat, xf, w, g, b)


def _ffn_ln_kernel(x_ref, xb_ref, wu_ref, wd_ref, g_ref, b_ref, o_ref, acc_ref, y_ref, *, alpha, nf):
    t = pl.program_id(0)

    def mlp():
        u = jnp.dot(xb_ref[...], wu_ref[...], preferred_element_type=F32)
        a = jnp.square(jnp.maximum(u, 0.0)).astype(BF16)
        return jnp.dot(a, wd_ref[...], preferred_element_type=F32)

    @pl.when(t == 0)
    def _():
        acc_ref[...] = jnp.zeros_like(acc_ref)

    @pl.when(t % nf == 0)
    def _():
        y_ref[...] = alpha * x_ref[...] + acc_ref[...]
        o_ref[...] = _layer_norm(y_ref[...], g_ref[...], b_ref[...])
        acc_ref[...] = mlp()

    @pl.when(t % nf != 0)
    def _():
        acc_ref[...] += mlp()


def _ffn_ln(x1, x1b, wu, wd, g, b, alpha):
    t, d = x1.shape
    dff = wu.shape[1]
    tm, tf = 512, 1024
    nt, nf = t // tm, dff // tf
    last = nt * nf - 1
    rows_now = lambda s: (jnp.minimum(s, last) // nf, 0)
    rows_late = lambda s: (jnp.maximum(s - 1, 0) // nf, 0)
    hidden = lambda s: jnp.minimum(s, last) % nf
    return pl.pallas_call(
        functools.partial(_ffn_ln_kernel, alpha=alpha, nf=nf),
        grid=(nt * nf + 1,),
        in_specs=[pl.BlockSpec((tm, d), rows_late),
                  pl.BlockSpec((tm, d), rows_now),
                  pl.BlockSpec((d, tf), lambda s: (0, hidden(s))),
                  pl.BlockSpec((tf, d), lambda s: (hidden(s), 0)),
                  pl.BlockSpec((1, d), lambda s: (0, 0)),
                  pl.BlockSpec((1, d), lambda s: (0, 0))],
        out_specs=pl.BlockSpec((tm, d), rows_late),
        out_shape=jax.ShapeDtypeStruct((t, d), F32),
        scratch_shapes=[pltpu.VMEM((tm, d), F32), pltpu.VMEM((tm, d), F32)],
        compiler_params=_params("arbitrary"),
        name="ffn_ln",
    )(x1, x1b, wu, wd, g, b)


def kernel(x, positions, w_in, q_norm_g, w_uq, kv_norm_g, w_uk, w_uv, ret_gn_g, w_out, ln1_g, ln1_b,
           w_up, w_down, ln2_g, ln2_b):
    batch, seq, d = x.shape
    depth = w_in.shape[0]
    t = batch * seq
    alpha = (2.0 * depth) ** 0.25
    xf = x.reshape(t, d)
    c_ret, s_ret, c_mla, sa_mla, sb_mla = _rope_tables(positions.reshape(t, 1))
    log_g = jnp.log1p(-jnp.exp2(-5.0 - jnp.arange(RET_HEADS, dtype=F32)))
    o_cq = 2 * RET_QK + 2 * RET_WIDTH
    o_ckv = o_cq + MLA_Q_RANK
    o_kr = o_ckv + MLA_KV_RANK
    for l in range(depth):
        w = w_in[l]
        w_cq = w[:, o_cq:o_ckv].astype(BF16)
        w_ckv = w[:, o_ckv:o_kr].astype(BF16)
        w_kr = jnp.pad(w[:, o_kr:], ((0, 0), (0, LANES - MLA_ROPE))).astype(BF16)
        wuq = w_uq[l].reshape(MLA_Q_RANK, MLA_HEADS, MLA_NOPE + MLA_ROPE)
        wuq = jnp.concatenate([wuq[:, :, :MLA_NOPE].reshape(MLA_Q_RANK, -1),
                               wuq[:, :, MLA_NOPE:].reshape(MLA_Q_RANK, -1)], axis=1).astype(BF16)

        qkvg = _proj(xf, w, c_ret, s_ret)
        ret = _retention(qkvg, ret_gn_g[l].reshape(1, RET_WIDTH), log_g, batch, seq)
        q, k, v = _mla_prep(xf, w_cq, w_ckv, w_kr, q_norm_g[l].reshape(1, -1),
                            kv_norm_g[l].reshape(1, -1), wuq, w_uk[l].astype(BF16),
                            w_uv[l].astype(BF16), c_mla, sa_mla, sb_mla)
        mla = _mla_attn(q, k, v, batch, seq)
        x1, x1b = _out_ln(ret, mla, xf, w_out[l], ln1_g[l].reshape(1, d),
                     ln1_b[l].reshape(1, d), alpha)
        xf = _ffn_ln(x1, x1b, w_up[l].astype(BF16), w_down[l].astype(BF16), ln2_g[l].reshape(1, d),
                     ln2_b[l].reshape(1, d), alpha)
    return xf.reshape(batch, seq, d)
```

```python
import functools

import jax
import jax.numpy as jnp
from jax import lax
from jax.experimental import pallas as pl
from jax.experimental.pallas import tpu as pltpu

F32 = jnp.float32
BF16 = jnp.bfloat16

LANES = 128
CHUNK = 64
RET_HEADS = 8
RET_DK = 128
RET_DV = 128
RET_QK = RET_HEADS * RET_DK
RET_WIDTH = RET_HEADS * RET_DV
MLA_HEADS = 8
MLA_NOPE = 128
MLA_ROPE = 64
MLA_DV = 128
MLA_Q_RANK = 768
MLA_KV_RANK = 512
MLA_QK_PAD = 256
MLA_WIDTH = MLA_HEADS * MLA_DV
MLA_Q_SCALE = (MLA_NOPE + MLA_ROPE) ** -0.5 * 1.4426950408889634
ROPE_BASE = 10000.0
EPS = 1e-5

VMEM_LIMIT = 56 * 1024 * 1024
OUT_LN_SPLIT = 2
PROJ_SPLIT = 2


def _params(*sem):
    return pltpu.CompilerParams(dimension_semantics=sem, vmem_limit_bytes=VMEM_LIMIT)


def _rope_table_kernel(pos_ref, cst_ref, cr_ref, sr_ref, cm_ref, sa_ref, sb_ref):
    pos = pos_ref[...].astype(F32)
    ang = pos * cst_ref[0:1, :]
    c = jnp.cos(ang)
    s = jnp.sin(ang)
    half = LANES // 2
    quarter = LANES // 4
    c64 = pltpu.roll(c, half, 1)
    s64 = pltpu.roll(s, half, 1)
    cr_ref[...] = c * cst_ref[1:2, :] + c64 * cst_ref[3:4, :]
    sr_ref[...] = s * cst_ref[2:3, :] + s64 * cst_ref[3:4, :]
    cm_ref[...] = c64 * cst_ref[4:5, :] + pltpu.roll(c, half + quarter, 1) * cst_ref[6:7, :]
    sa_ref[...] = s64 * cst_ref[5:6, :]
    sb_ref[...] = pltpu.roll(s, half + quarter, 1) * cst_ref[6:7, :]


def _rope_tables(pos):
    t = pos.shape[0]
    tm = 2048
    half_r = RET_DK // 2
    half_m = MLA_ROPE // 2
    inv_r = ROPE_BASE ** (-jnp.arange(0, RET_DK, 2, dtype=F32) / RET_DK)
    inv_m = ROPE_BASE ** (-jnp.arange(0, MLA_ROPE, 2, dtype=F32) / MLA_ROPE)
    assert half_r == LANES // 2 and half_m == LANES // 4
    lane = jnp.arange(LANES)
    lo64 = (lane < half_r).astype(F32)
    q0 = (lane < half_m).astype(F32)
    q1 = ((lane >= half_m) & (lane < 2 * half_m)).astype(F32)
    cst = jnp.stack([
        jnp.concatenate([inv_r, inv_m, jnp.zeros((LANES - half_r - half_m,), F32)]),
        lo64, -lo64, 1.0 - lo64, q0, -q0, q1, jnp.zeros((LANES,), F32)])
    out = jax.ShapeDtypeStruct((t, LANES), F32)
    tab_spec = pl.BlockSpec((tm, LANES), lambda i: (i, 0))
    return pl.pallas_call(
        _rope_table_kernel,
        grid=(t // tm,),
        in_specs=[pl.BlockSpec((tm, 1), lambda i: (i, 0)),
                  pl.BlockSpec((8, LANES), lambda i: (0, 0))],
        out_specs=[tab_spec] * 5,
        out_shape=[out] * 5,
        compiler_params=_params("parallel"),
        name="rope_tables",
    )(pos, cst)


def _proj_kernel(x_ref, w_ref, c_ref, s_ref, o_ref, xb_ref, *, k_scale):
    j = pl.program_id(0)
    rotary = j < 2
    scale = jnp.where(j == 1, k_scale, 1.0).astype(F32)
    rows = x_ref.shape[0] // PROJ_SPLIT
    for r0 in range(0, x_ref.shape[0], rows):
        sl = slice(r0, r0 + rows)
        xb = x_ref[sl, :].astype(xb_ref.dtype)
        res = jnp.dot(xb, w_ref[...], preferred_element_type=F32)
        c = jnp.where(rotary, c_ref[sl, :] * scale, 1.0)
        s = jnp.where(rotary, s_ref[sl, :] * scale, 0.0)
        for h in range(res.shape[1] // RET_DK):
            r = res[:, h * RET_DK:(h + 1) * RET_DK]
            o = r * c + pltpu.roll(r, RET_DK // 2, 1) * s
            o_ref[sl, h * RET_DK:(h + 1) * RET_DK] = o.astype(o_ref.dtype)

    @pl.when(j == 0)
    def _():
        xb_ref[...] = x_ref[...].astype(xb_ref.dtype)


def _proj(xf, w, c_tab, s_tab):
    t, d = xf.shape
    tm, tn = 1024, RET_QK
    n = 4 * tn
    nt = t // tm
    assert RET_QK == RET_WIDTH and w.shape[1] >= n
    rows = lambda j, i: (i, 0)
    xb_rows = lambda j, i: (jnp.where(j == 0, i, nt - 1), 0)
    return pl.pallas_call(
        functools.partial(_proj_kernel, k_scale=RET_DK ** -0.5),
        grid=(n // tn, nt),
        in_specs=[pl.BlockSpec((tm, d), rows),
                  pl.BlockSpec((d, tn), lambda j, i: (0, j)),
                  pl.BlockSpec((tm, LANES), rows),
                  pl.BlockSpec((tm, LANES), rows)],
        out_specs=[pl.BlockSpec((tm, tn), lambda j, i: (i, j)),
                   pl.BlockSpec((tm, d), xb_rows)],
        out_shape=[jax.ShapeDtypeStruct((t, n), BF16), jax.ShapeDtypeStruct((t, d), BF16)],
        compiler_params=_params("arbitrary", "arbitrary"),
        name="proj",
    )(xf, w, c_tab, s_tab)


def _retention_kernel(lg_ref, q_ref, k_ref, v_ref, g_ref, gn_ref, o_ref, d_ref, *, blk):
    lg = lg_ref[pl.program_id(1)]
    row = lax.broadcasted_iota(jnp.int32, (blk, blk), 0)
    col = lax.broadcasted_iota(jnp.int32, (blk, blk), 1)
    dist = jnp.abs(row - col).astype(F32)
    d_ref[...] = jnp.where(col // CHUNK <= row // CHUNK, jnp.exp(lg * dist), 0.0)
    idx = lax.broadcasted_iota(jnp.int32, (blk, 1), 0).astype(F32)
    q_dec = jnp.exp(lg * (idx + 1.0))
    k_dec = jnp.exp(lg * (blk - 1.0 - idx))
    blk_dec = jnp.exp(jnp.full((1, 1), blk, F32) * lg)
    gn = gn_ref[...]
    nt = (((1,), (1,)), ((), ()))
    tn = (((0,), (0,)), ((), ()))

    def body(n, state):
        sl = pl.ds(pl.multiple_of(n * blk, blk), blk)
        q = q_ref[sl, :]
        k = k_ref[sl, :]
        v = v_ref[sl, :]
        s = lax.dot_general(q, k, nt, preferred_element_type=F32) * d_ref[...]
        intra = jnp.dot(s.astype(BF16), v, preferred_element_type=F32)
        cross = jnp.dot(q, state.astype(BF16), preferred_element_type=F32) * q_dec
        kd = (k.astype(F32) * k_dec).astype(BF16)
        new_state = blk_dec * state + lax.dot_general(kd, v, tn, preferred_element_type=F32)
        ret = intra + cross
        mu = jnp.mean(ret, axis=-1, keepdims=True)
        dev = ret - mu
        var = jnp.mean(dev * dev, axis=-1, keepdims=True)
        y = dev * lax.rsqrt(var + EPS) * gn
        g = g_ref[sl, :].astype(F32)
        gate = g / (1.0 + jnp.exp(-g))
        o_ref[sl, :] = (gate * y).astype(o_ref.dtype)
        return new_state

    lax.fori_loop(0, q_ref.shape[0] // blk, body, jnp.zeros((RET_DK, RET_DV), F32), unroll=True)


def _retention(qkvg, gn_g, log_g, batch, seq):
    t = qkvg.shape[0]
    blk = 256
    h = RET_HEADS
    return pl.pallas_call(
        functools.partial(_retention_kernel, blk=blk),
        grid=(batch, h),
        in_specs=[pl.BlockSpec(memory_space=pltpu.SMEM),
                  pl.BlockSpec((seq, RET_DK), lambda b, i: (b, i)),
                  pl.BlockSpec((seq, RET_DK), lambda b, i: (b, h + i)),
                  pl.BlockSpec((seq, RET_DV), lambda b, i: (b, 2 * h + i)),
                  pl.BlockSpec((seq, RET_DV), lambda b, i: (b, 3 * h + i)),
                  pl.BlockSpec((1, RET_DV), lambda b, i: (0, i))],
        out_specs=pl.BlockSpec((seq, RET_DV), lambda b, i: (b, i)),
        out_shape=jax.ShapeDtypeStruct((t, RET_WIDTH), BF16),
        scratch_shapes=[pltpu.VMEM((blk, blk), F32)],
        compiler_params=_params("parallel", "arbitrary"),
        name="retention",
    )(log_g, qkvg, qkvg, qkvg, qkvg, gn_g)


def _rms(v, g):
    return v * lax.rsqrt(jnp.mean(v * v, axis=-1, keepdims=True) + EPS) * g


def _rope64(r, cm, sa, sb):
    half = MLA_ROPE // 2
    return r * cm + pltpu.roll(r, LANES - half, 1) * sa + pltpu.roll(r, half, 1) * sb


def _mla_prep_kernel(x_ref, wcq_ref, wckv_ref, wkr_ref, qg_ref, kvg_ref, wuq_ref, wuk_ref, wuv_ref,
                     cm_ref, sa_ref, sb_ref, q_ref, k_ref, v_ref):
    x = x_ref[...]
    cm = cm_ref[...]
    sa = sa_ref[...]
    sb = sb_ref[...]
    cq = jnp.dot(x, wcq_ref[...], preferred_element_type=F32)
    cqn = _rms(cq, qg_ref[...]).astype(BF16)
    q = jnp.dot(cqn, wuq_ref[...], preferred_element_type=F32) * MLA_Q_SCALE
    rope0 = MLA_HEADS * MLA_NOPE
    for h in range(MLA_HEADS):
        lo = h * MLA_QK_PAD
        q_ref[:, lo:lo + MLA_NOPE] = q[:, h * MLA_NOPE:(h + 1) * MLA_NOPE].astype(q_ref.dtype)
        pair = q[:, rope0 + (h // 2) * LANES:rope0 + (h // 2 + 1) * LANES]
        if h % 2:
            pair = pltpu.roll(pair, LANES // 2, 1)
        q_ref[:, lo + MLA_NOPE:lo + MLA_QK_PAD] = _rope64(pair, cm, sa, sb).astype(q_ref.dtype)
    ckv = jnp.dot(x, wckv_ref[...], preferred_element_type=F32)
    ckvn = _rms(ckv, kvg_ref[...]).astype(BF16)
    kr = jnp.dot(x, wkr_ref[...], preferred_element_type=F32)
    k_rope = _rope64(kr, cm, sa, sb).astype(k_ref.dtype)
    kn = jnp.dot(ckvn, wuk_ref[...], preferred_element_type=F32)
    for h in range(MLA_HEADS):
        lo = h * MLA_QK_PAD
        k_ref[:, lo:lo + MLA_NOPE] = kn[:, h * MLA_NOPE:(h + 1) * MLA_NOPE].astype(k_ref.dtype)
        k_ref[:, lo + MLA_NOPE:lo + MLA_QK_PAD] = k_rope
    v_ref[...] = jnp.dot(ckvn, wuv_ref[...], preferred_element_type=F32).astype(v_ref.dtype)


def _mla_prep(xb, wcq, wckv, wkr, qg, kvg, wuq, wuk, wuv, cm, sa, sb):
    t, d = xb.shape
    tm = 1024
    row = lambda i: (i, 0)
    whole = lambda i: (0, 0)
    full = lambda a: pl.BlockSpec(a.shape, whole, pipeline_mode=pl.Buffered(1))
    qk_w = MLA_HEADS * MLA_QK_PAD
    return pl.pallas_call(
        _mla_prep_kernel,
        grid=(t // tm,),
        in_specs=[pl.BlockSpec((tm, d), row), full(wcq), full(wckv), full(wkr), full(qg), full(kvg),
                  full(wuq), full(wuk), full(wuv),
                  pl.BlockSpec((tm, LANES), row), pl.BlockSpec((tm, LANES), row),
                  pl.BlockSpec((tm, LANES), row)],
        out_specs=[pl.BlockSpec((tm, qk_w), row), pl.BlockSpec((tm, qk_w), row),
                   pl.BlockSpec((tm, MLA_WIDTH), row)],
        out_shape=[jax.ShapeDtypeStruct((t, qk_w), BF16), jax.ShapeDtypeStruct((t, qk_w), BF16),
                   jax.ShapeDtypeStruct((t, MLA_WIDTH), BF16)],
        compiler_params=_params("parallel"),
        name="mla_prep",
    )(xb, wcq, wckv, wkr, qg, kvg, wuq, wuk, wuv, cm, sa, sb)


def _mla_attn_kernel(q_ref, k_ref, v_ref, o_ref, sa_ref, sb_ref, pa_ref, pb_ref, acc_ref, qt_ref, *,
                     qblk, kblk):
    tn = (((0,), (0,)), ((), ()))

    def keys(j):
        return pl.ds(pl.multiple_of(j * kblk, kblk), kblk)

    everything = slice(None)

    def scores(s_ref, j, cols=everything):
        s_ref[:, cols] = jnp.dot(k_ref[keys(j), :], qt_ref[:, cols], preferred_element_type=F32)

    def probs(s_ref, p_ref, stats, cols=everything, masked=False):
        m, l = stats
        st = s_ref[:, cols]
        if masked:
            key = lax.broadcasted_iota(jnp.int32, st.shape, 0)
            qry = lax.broadcasted_iota(jnp.int32, st.shape, 1)
            st = jnp.where(key // CHUNK <= qry // CHUNK, st, -jnp.inf)
        m_new = jnp.maximum(m, jnp.max(st, axis=0, keepdims=True))
        alpha = jnp.exp2(m - m_new)
        p = jnp.exp2(st - m_new)
        p_ref[:, cols] = p.astype(p_ref.dtype)
        return (m_new, alpha * l + jnp.sum(p, axis=0, keepdims=True)), alpha

    def accum(p_ref, j, alpha, cols=everything):
        pv = lax.dot_general(v_ref[keys(j), :], p_ref[:, cols], tn, preferred_element_type=F32)
        acc_ref[:, cols] = alpha * acc_ref[:, cols] + pv

    def body(jj, carry):
        stats, alpha = carry
        j = 2 * jj
        scores(sb_ref, j + 1)
        accum(pb_ref, jnp.maximum(j - 1, 0), alpha)
        stats, alpha = probs(sa_ref, pa_ref, stats)
        scores(sa_ref, j + 2)
        accum(pa_ref, j, alpha)
        stats, alpha = probs(sb_ref, pb_ref, stats)
        return stats, alpha

    nq = q_ref.shape[0] // qblk

    def query_rows(qi):
        return pl.ds(pl.multiple_of(qi * qblk, qblk), qblk)

    def start_query_block(qi):
        qt_ref[...] = q_ref[query_rows(qi), :].T
        scores(sa_ref, 0)

    def query_block(qi, _):
        acc_ref[...] = jnp.zeros_like(acc_ref)
        pb_ref[...] = jnp.zeros_like(pb_ref)
        stats = (jnp.full((1, qblk), -jnp.inf, F32), jnp.zeros((1, qblk), F32))
        nfull = qi * (qblk // kblk)
        stats, alpha = lax.fori_loop(0, nfull // 2, body, (stats, jnp.ones((1, qblk), F32)))
        late = slice(kblk, qblk)
        scores(sb_ref, nfull + 1, late)
        accum(pb_ref, jnp.maximum(nfull - 1, 0), alpha)
        (m, l), alpha = probs(sa_ref, pa_ref, stats, masked=True)
        start_query_block(jnp.minimum(qi + 1, nq - 1))
        accum(pa_ref, nfull, alpha)
        (_, l_late), alpha = probs(sb_ref, pb_ref, (m[:, late], l[:, late]), late, masked=True)
        accum(pb_ref, nfull + 1, alpha, late)
        l = jnp.concatenate([l[:, :kblk], l_late], axis=1)
        o_ref[:, query_rows(qi)] = (acc_ref[...] / l).astype(o_ref.dtype)
        return 0

    start_query_block(0)
    lax.fori_loop(0, nq, query_block, 0)


def _mla_attn(q, k, v, batch, seq):
    t = q.shape[0]
    qblk = 1024
    kblk = qblk // 2
    per_head = lambda b, h: (b, h)
    return pl.pallas_call(
        functools.partial(_mla_attn_kernel, qblk=qblk, kblk=kblk),
        grid=(batch, MLA_HEADS),
        in_specs=[pl.BlockSpec((seq, MLA_QK_PAD), per_head),
                  pl.BlockSpec((seq, MLA_QK_PAD), per_head),
                  pl.BlockSpec((seq, MLA_DV), per_head)],
        out_specs=pl.BlockSpec((MLA_DV, seq), lambda b, h: (h, b)),
        out_shape=jax.ShapeDtypeStruct((MLA_WIDTH, t), BF16),
        scratch_shapes=[pltpu.VMEM((kblk, qblk), F32), pltpu.VMEM((kblk, qblk), F32),
                        pltpu.VMEM((kblk, qblk), BF16), pltpu.VMEM((kblk, qblk), BF16),
                        pltpu.VMEM((MLA_DV, qblk), F32), pltpu.VMEM((MLA_QK_PAD, qblk), BF16)],
        compiler_params=_params("parallel", "arbitrary"),
        name="mla_attn",
    )(q, k, v)


def _layer_norm(y, g, b):
    mu = jnp.mean(y, axis=-1, keepdims=True)
    dev = y - mu
    var = jnp.mean(dev * dev, axis=-1, keepdims=True)
    return dev * lax.rsqrt(var + EPS) * g + b


def _out_ln_kernel(ret_ref, mlat_ref, x_ref, w_ref, g_ref, b_ref, o_ref, ob_ref, wb_ref, *, alpha):
    @pl.when(pl.program_id(0) == 0)
    def _():
        wb_ref[...] = w_ref[...].astype(wb_ref.dtype)

    kr = ret_ref.shape[1]
    rows = x_ref.shape[0] // OUT_LN_SPLIT
    for r in range(OUT_LN_SPLIT):
        sl = slice(r * rows, (r + 1) * rows)
        mix = jnp.dot(ret_ref[sl, :], wb_ref[0:kr, :], preferred_element_type=F32)
        mix = mix + lax.dot_general(mlat_ref[:, sl], wb_ref[kr:, :], (((0,), (0,)), ((), ())),
                                    preferred_element_type=F32)
        y = alpha * x_ref[sl, :] + mix
        out = _layer_norm(y, g_ref[...], b_ref[...])
        o_ref[sl, :] = out
        ob_ref[sl, :] = out.astype(ob_ref.dtype)


def _out_ln(ret, mlat, xf, w, g, b, alpha):
    t, d = xf.shape
    tm = 512
    row = lambda i: (i, 0)
    whole = lambda i: (0, 0)
    return pl.pallas_call(
        functools.partial(_out_ln_kernel, alpha=alpha),
        grid=(t // tm,),
        in_specs=[pl.BlockSpec((tm, ret.shape[1]), row),
                  pl.BlockSpec((mlat.shape[0], tm), lambda i: (0, i)),
                  pl.BlockSpec((tm, d), row),
                  pl.BlockSpec(w.shape, whole, pipeline_mode=pl.Buffered(1)),
                  pl.BlockSpec((1, d), whole), pl.BlockSpec((1, d), whole)],
        out_specs=[pl.BlockSpec((tm, d), row), pl.BlockSpec((tm, d), row)],
        out_shape=[jax.ShapeDtypeStruct((t, d), F32), jax.ShapeDtypeStruct((t, d), BF16)],
        scratch_shapes=[pltpu.VMEM(w.shape, BF16)],
        compiler_params=_params("arbitrary"),
        name="out_ln",
    )(ret, mlat, xf, w, g, b)


def _ffn_ln_kernel(x_ref, xb_ref, wu_ref, wd_ref, g_ref, b_ref, o_ref, acc_ref, y_ref, *, alpha, nf):
    t = pl.program_id(0)

    def mlp():
        u = jnp.dot(xb_ref[...], wu_ref[...], preferred_element_type=F32)
        a = jnp.square(jnp.maximum(u, 0.0)).astype(BF16)
        return jnp.dot(a, wd_ref[...], preferred_element_type=F32)

    @pl.when(t == 0)
    def _():
        acc_ref[...] = jnp.zeros_like(acc_ref)

    @pl.when(t % nf == 0)
    def _():
        y_ref[...] = alpha * x_ref[...] + acc_ref[...]
        o_ref[...] = _layer_norm(y_ref[...], g_ref[...], b_ref[...])
        acc_ref[...] = mlp()

    @pl.when(t % nf != 0)
    def _():
        acc_ref[...] += mlp()


def _ffn_ln(x1, x1b, wu, wd, g, b, alpha):
    t, d = x1.shape
    dff = wu.shape[1]
    tm, tf = 512, 1024
    nt, nf = t // tm, dff // tf
    last = nt * nf - 1
    rows_now = lambda s: (jnp.minimum(s, last) // nf, 0)
    rows_late = lambda s: (jnp.maximum(s - 1, 0) // nf, 0)
    hidden = lambda s: jnp.minimum(s, last) % nf
    return pl.pallas_call(
        functools.partial(_ffn_ln_kernel, alpha=alpha, nf=nf),
        grid=(nt * nf + 1,),
        in_specs=[pl.BlockSpec((tm, d), rows_late),
                  pl.BlockSpec((tm, d), rows_now),
                  pl.BlockSpec((d, tf), lambda s: (0, hidden(s))),
                  pl.BlockSpec((tf, d), lambda s: (hidden(s), 0)),
                  pl.BlockSpec((1, d), lambda s: (0, 0)),
                  pl.BlockSpec((1, d), lambda s: (0, 0))],
        out_specs=pl.BlockSpec((tm, d), rows_late),
        out_shape=jax.ShapeDtypeStruct((t, d), F32),
        scratch_shapes=[pltpu.VMEM((tm, d), F32), pltpu.VMEM((tm, d), F32)],
        compiler_params=_params("arbitrary"),
        name="ffn_ln",
    )(x1, x1b, wu, wd, g, b)


def kernel(x, positions, w_in, q_norm_g, w_uq, kv_norm_g, w_uk, w_uv, ret_gn_g, w_out, ln1_g, ln1_b,
           w_up, w_down, ln2_g, ln2_b):
    batch, seq, d = x.shape
    depth = w_in.shape[0]
    t = batch * seq
    alpha = (2.0 * depth) ** 0.25
    xf = x.reshape(t, d)
    c_ret, s_ret, c_mla, sa_mla, sb_mla = _rope_tables(positions.reshape(t, 1))
    log_g = jnp.log1p(-jnp.exp2(-5.0 - jnp.arange(RET_HEADS, dtype=F32)))
    o_cq = 2 * RET_QK + 2 * RET_WIDTH
    o_ckv = o_cq + MLA_Q_RANK
    o_kr = o_ckv + MLA_KV_RANK
    for l in range(depth):
        w = w_in[l].astype(BF16)
        w_cq = w[:, o_cq:o_ckv]
        w_ckv = w[:, o_ckv:o_kr]
        w_kr = jnp.pad(w[:, o_kr:], ((0, 0), (0, LANES - MLA_ROPE)))
        wuq = w_uq[l].reshape(MLA_Q_RANK, MLA_HEADS, MLA_NOPE + MLA_ROPE)
        wuq = jnp.concatenate([wuq[:, :, :MLA_NOPE].reshape(MLA_Q_RANK, -1),
                               wuq[:, :, MLA_NOPE:].reshape(MLA_Q_RANK, -1)], axis=1).astype(BF16)

        qkvg, xb = _proj(xf, w, c_ret, s_ret)
        ret = _retention(qkvg, ret_gn_g[l].reshape(1, RET_WIDTH), log_g, batch, seq)
        q, k, v = _mla_prep(xb, w_cq, w_ckv, w_kr, q_norm_g[l].reshape(1, -1),
                            kv_norm_g[l].reshape(1, -1), wuq, w_uk[l].astype(BF16),
                            w_uv[l].astype(BF16), c_mla, sa_mla, sb_mla)
        mla = _mla_attn(q, k, v, batch, seq)
        x1, x1b = _out_ln(ret, mla, xf, w_out[l], ln1_g[l].reshape(1, d),
                     ln1_b[l].reshape(1, d), alpha)
        xf = _ffn_ln(x1, x1b, w_up[l].astype(BF16), w_down[l].astype(BF16), ln2_g[l].reshape(1, d),
                     ln2_b[l].reshape(1, d), alpha)
    return xf.reshape(batch, seq, d)
```

```python
import functools

import jax
import jax.numpy as jnp
from jax import lax
from jax.experimental import pallas as pl
from jax.experimental.pallas import tpu as pltpu

F32 = jnp.float32
BF16 = jnp.bfloat16

LANES = 128
CHUNK = 64
RET_HEADS = 8
RET_DK = 128
RET_DV = 128
RET_QK = RET_HEADS * RET_DK
RET_WIDTH = RET_HEADS * RET_DV
MLA_HEADS = 8
MLA_NOPE = 128
MLA_ROPE = 64
MLA_DV = 128
MLA_Q_RANK = 768
MLA_KV_RANK = 512
MLA_QK_PAD = 256
MLA_WIDTH = MLA_HEADS * MLA_DV
MLA_Q_SCALE = (MLA_NOPE + MLA_ROPE) ** -0.5 * 1.4426950408889634
ROPE_BASE = 10000.0
EPS = 1e-5

VMEM_LIMIT = 56 * 1024 * 1024
OUT_LN_SPLIT = 2
PROJ_SPLIT = 2


def _params(*sem):
    return pltpu.CompilerParams(dimension_semantics=sem, vmem_limit_bytes=VMEM_LIMIT)


def _rope_table_kernel(pos_ref, cst_ref, cr_ref, sr_ref, cm_ref, sa_ref, sb_ref):
    pos = pos_ref[...].astype(F32)
    ang = pos * cst_ref[0:1, :]
    c = jnp.cos(ang)
    s = jnp.sin(ang)
    half = LANES // 2
    quarter = LANES // 4
    c64 = pltpu.roll(c, half, 1)
    s64 = pltpu.roll(s, half, 1)
    cr_ref[...] = c * cst_ref[1:2, :] + c64 * cst_ref[3:4, :]
    sr_ref[...] = s * cst_ref[2:3, :] + s64 * cst_ref[3:4, :]
    cm_ref[...] = c64 * cst_ref[4:5, :] + pltpu.roll(c, half + quarter, 1) * cst_ref[6:7, :]
    sa_ref[...] = s64 * cst_ref[5:6, :]
    sb_ref[...] = pltpu.roll(s, half + quarter, 1) * cst_ref[6:7, :]


def _rope_tables(pos):
    t = pos.shape[0]
    tm = 2048
    half_r = RET_DK // 2
    half_m = MLA_ROPE // 2
    inv_r = ROPE_BASE ** (-jnp.arange(0, RET_DK, 2, dtype=F32) / RET_DK)
    inv_m = ROPE_BASE ** (-jnp.arange(0, MLA_ROPE, 2, dtype=F32) / MLA_ROPE)
    assert half_r == LANES // 2 and half_m == LANES // 4
    lane = jnp.arange(LANES)
    lo64 = (lane < half_r).astype(F32)
    q0 = (lane < half_m).astype(F32)
    q1 = ((lane >= half_m) & (lane < 2 * half_m)).astype(F32)
    cst = jnp.stack([
        jnp.concatenate([inv_r, inv_m, jnp.zeros((LANES - half_r - half_m,), F32)]),
        lo64, -lo64, 1.0 - lo64, q0, -q0, q1, jnp.zeros((LANES,), F32)])
    out = jax.ShapeDtypeStruct((t, LANES), F32)
    tab_spec = pl.BlockSpec((tm, LANES), lambda i: (i, 0))
    return pl.pallas_call(
        _rope_table_kernel,
        grid=(t // tm,),
        in_specs=[pl.BlockSpec((tm, 1), lambda i: (i, 0)),
                  pl.BlockSpec((8, LANES), lambda i: (0, 0))],
        out_specs=[tab_spec] * 5,
        out_shape=[out] * 5,
        compiler_params=_params("parallel"),
        name="rope_tables",
    )(pos, cst)


def _proj_kernel(x_ref, wt_ref, c_ref, s_ref, o_ref, xb_ref, wb_ref, *, k_scale):
    j = pl.program_id(0)

    @pl.when(pl.program_id(1) == 0)
    def _():
        wb_ref[...] = wt_ref[...].astype(wb_ref.dtype)

    rotary = j < 2
    scale = jnp.where(j == 1, k_scale, 1.0).astype(F32)
    rows = x_ref.shape[0] // PROJ_SPLIT
    nt_dims = (((1,), (1,)), ((), ()))
    for r0 in range(0, x_ref.shape[0], rows):
        sl = slice(r0, r0 + rows)
        xb = x_ref[sl, :].astype(xb_ref.dtype)
        res = lax.dot_general(xb, wb_ref[...], nt_dims, preferred_element_type=F32)
        c = jnp.where(rotary, c_ref[sl, :] * scale, 1.0)
        s = jnp.where(rotary, s_ref[sl, :] * scale, 0.0)
        for h in range(res.shape[1] // RET_DK):
            r = res[:, h * RET_DK:(h + 1) * RET_DK]
            o = r * c + pltpu.roll(r, RET_DK // 2, 1) * s
            o_ref[sl, h * RET_DK:(h + 1) * RET_DK] = o.astype(o_ref.dtype)

    @pl.when(j == 0)
    def _():
        xb_ref[...] = x_ref[...].astype(xb_ref.dtype)


def _proj(xf, wt, c_tab, s_tab):
    t, d = xf.shape
    tm, tn = 1024, RET_QK
    n = 4 * tn
    nt = t // tm
    assert RET_QK == RET_WIDTH and wt.shape[0] >= n
    rows = lambda j, i: (i, 0)
    xb_rows = lambda j, i: (jnp.where(j == 0, i, nt - 1), 0)
    return pl.pallas_call(
        functools.partial(_proj_kernel, k_scale=RET_DK ** -0.5),
        grid=(n // tn, nt),
        in_specs=[pl.BlockSpec((tm, d), rows),
                  pl.BlockSpec((tn, d), lambda j, i: (j, 0)),
                  pl.BlockSpec((tm, LANES), rows),
                  pl.BlockSpec((tm, LANES), rows)],
        out_specs=[pl.BlockSpec((tm, tn), lambda j, i: (i, j)),
                   pl.BlockSpec((tm, d), xb_rows)],
        out_shape=[jax.ShapeDtypeStruct((t, n), BF16), jax.ShapeDtypeStruct((t, d), BF16)],
        scratch_shapes=[pltpu.VMEM((tn, d), BF16)],
        compiler_params=_params("arbitrary", "arbitrary"),
        name="proj",
    )(xf, wt, c_tab, s_tab)


def _retention_kernel(lg_ref, q_ref, k_ref, v_ref, g_ref, gn_ref, o_ref, d_ref, *, blk):
    lg = lg_ref[pl.program_id(1)]
    row = lax.broadcasted_iota(jnp.int32, (blk, blk), 0)
    col = lax.broadcasted_iota(jnp.int32, (blk, blk), 1)
    dist = jnp.abs(row - col).astype(F32)
    d_ref[...] = jnp.where(col // CHUNK <= row // CHUNK, jnp.exp(lg * dist), 0.0)
    idx = lax.broadcasted_iota(jnp.int32, (blk, 1), 0).astype(F32)
    q_dec = jnp.exp(lg * (idx + 1.0))
    k_dec = jnp.exp(lg * (blk - 1.0 - idx))
    blk_dec = jnp.exp(jnp.full((1, 1), blk, F32) * lg)
    gn = gn_ref[...]
    nt = (((1,), (1,)), ((), ()))
    tn = (((0,), (0,)), ((), ()))

    def body(n, state):
        sl = pl.ds(pl.multiple_of(n * blk, blk), blk)
        q = q_ref[sl, :]
        k = k_ref[sl, :]
        v = v_ref[sl, :]
        s = lax.dot_general(q, k, nt, preferred_element_type=F32) * d_ref[...]
        intra = jnp.dot(s.astype(BF16), v, preferred_element_type=F32)
        cross = jnp.dot(q, state.astype(BF16), preferred_element_type=F32) * q_dec
        kd = (k.astype(F32) * k_dec).astype(BF16)
        new_state = blk_dec * state + lax.dot_general(kd, v, tn, preferred_element_type=F32)
        ret = intra + cross
        mu = jnp.mean(ret, axis=-1, keepdims=True)
        dev = ret - mu
        var = jnp.mean(dev * dev, axis=-1, keepdims=True)
        y = dev * lax.rsqrt(var + EPS) * gn
        g = g_ref[sl, :].astype(F32)
        gate = g / (1.0 + jnp.exp(-g))
        o_ref[sl, :] = (gate * y).astype(o_ref.dtype)
        return new_state

    lax.fori_loop(0, q_ref.shape[0] // blk, body, jnp.zeros((RET_DK, RET_DV), F32), unroll=True)


def _retention(qkvg, gn_g, log_g, batch, seq):
    t = qkvg.shape[0]
    blk = 256
    h = RET_HEADS
    return pl.pallas_call(
        functools.partial(_retention_kernel, blk=blk),
        grid=(batch, h),
        in_specs=[pl.BlockSpec(memory_space=pltpu.SMEM),
                  pl.BlockSpec((seq, RET_DK), lambda b, i: (b, i)),
                  pl.BlockSpec((seq, RET_DK), lambda b, i: (b, h + i)),
                  pl.BlockSpec((seq, RET_DV), lambda b, i: (b, 2 * h + i)),
                  pl.BlockSpec((seq, RET_DV), lambda b, i: (b, 3 * h + i)),
                  pl.BlockSpec((1, RET_DV), lambda b, i: (0, i))],
        out_specs=pl.BlockSpec((seq, RET_DV), lambda b, i: (b, i)),
        out_shape=jax.ShapeDtypeStruct((t, RET_WIDTH), BF16),
        scratch_shapes=[pltpu.VMEM((blk, blk), F32)],
        compiler_params=_params("parallel", "arbitrary"),
        name="retention",
    )(log_g, qkvg, qkvg, qkvg, qkvg, gn_g)


def _rms(v, g):
    return v * lax.rsqrt(jnp.mean(v * v, axis=-1, keepdims=True) + EPS) * g


def _rope64(r, cm, sa, sb):
    half = MLA_ROPE // 2
    return r * cm + pltpu.roll(r, LANES - half, 1) * sa + pltpu.roll(r, half, 1) * sb


def _mla_prep_kernel(x_ref, wc_ref, qg_ref, kvg_ref, wuq_ref, wuk_ref, wuv_ref,
                     cm_ref, sa_ref, sb_ref, q_ref, k_ref, v_ref):
    r_ckv = MLA_Q_RANK
    r_kr = r_ckv + MLA_KV_RANK
    wcq_ref = wc_ref.at[0:r_ckv, :]
    wckv_ref = wc_ref.at[r_ckv:r_kr, :]
    wkr_ref = wc_ref.at[r_kr:r_kr + LANES, :]
    x = x_ref[...]
    cm = cm_ref[...]
    sa = sa_ref[...]
    sb = sb_ref[...]
    nt = (((1,), (1,)), ((), ()))
    cq = lax.dot_general(x, wcq_ref[...], nt, preferred_element_type=F32)
    cqn = _rms(cq, qg_ref[...]).astype(BF16)
    q = jnp.dot(cqn, wuq_ref[...], preferred_element_type=F32) * MLA_Q_SCALE
    rope0 = MLA_HEADS * MLA_NOPE
    for h in range(MLA_HEADS):
        lo = h * MLA_QK_PAD
        q_ref[:, lo:lo + MLA_NOPE] = q[:, h * MLA_NOPE:(h + 1) * MLA_NOPE].astype(q_ref.dtype)
        pair = q[:, rope0 + (h // 2) * LANES:rope0 + (h // 2 + 1) * LANES]
        if h % 2:
            pair = pltpu.roll(pair, LANES // 2, 1)
        q_ref[:, lo + MLA_NOPE:lo + MLA_QK_PAD] = _rope64(pair, cm, sa, sb).astype(q_ref.dtype)
    ckv = lax.dot_general(x, wckv_ref[...], nt, preferred_element_type=F32)
    ckvn = _rms(ckv, kvg_ref[...]).astype(BF16)
    kr = lax.dot_general(x, wkr_ref[...], nt, preferred_element_type=F32)
    k_rope = _rope64(kr, cm, sa, sb).astype(k_ref.dtype)
    kn = jnp.dot(ckvn, wuk_ref[...], preferred_element_type=F32)
    for h in range(MLA_HEADS):
        lo = h * MLA_QK_PAD
        k_ref[:, lo:lo + MLA_NOPE] = kn[:, h * MLA_NOPE:(h + 1) * MLA_NOPE].astype(k_ref.dtype)
        k_ref[:, lo + MLA_NOPE:lo + MLA_QK_PAD] = k_rope
    v_ref[...] = jnp.dot(ckvn, wuv_ref[...], preferred_element_type=F32).astype(v_ref.dtype)


def _mla_prep(xb, wc, qg, kvg, wuq, wuk, wuv, cm, sa, sb):
    t, d = xb.shape
    tm = 1024
    row = lambda i: (i, 0)
    whole = lambda i: (0, 0)
    full = lambda a: pl.BlockSpec(a.shape, whole, pipeline_mode=pl.Buffered(1))
    qk_w = MLA_HEADS * MLA_QK_PAD
    return pl.pallas_call(
        _mla_prep_kernel,
        grid=(t // tm,),
        in_specs=[pl.BlockSpec((tm, d), row), full(wc), full(qg), full(kvg),
                  full(wuq), full(wuk), full(wuv),
                  pl.BlockSpec((tm, LANES), row), pl.BlockSpec((tm, LANES), row),
                  pl.BlockSpec((tm, LANES), row)],
        out_specs=[pl.BlockSpec((tm, qk_w), row), pl.BlockSpec((tm, qk_w), row),
                   pl.BlockSpec((tm, MLA_WIDTH), row)],
        out_shape=[jax.ShapeDtypeStruct((t, qk_w), BF16), jax.ShapeDtypeStruct((t, qk_w), BF16),
                   jax.ShapeDtypeStruct((t, MLA_WIDTH), BF16)],
        compiler_params=_params("parallel"),
        name="mla_prep",
    )(xb, wc, qg, kvg, wuq, wuk, wuv, cm, sa, sb)


def _mla_attn_kernel(q_ref, k_ref, v_ref, o_ref, sa_ref, sb_ref, pa_ref, pb_ref, acc_ref, qt_ref, *,
                     qblk, kblk):
    tn = (((0,), (0,)), ((), ()))

    def keys(j):
        return pl.ds(pl.multiple_of(j * kblk, kblk), kblk)

    everything = slice(None)

    def scores(s_ref, j, cols=everything):
        s_ref[:, cols] = jnp.dot(k_ref[keys(j), :], qt_ref[:, cols], preferred_element_type=F32)

    def probs(s_ref, p_ref, stats, cols=everything, masked=False):
        m, l = stats
        st = s_ref[:, cols]
        if masked:
            key = lax.broadcasted_iota(jnp.int32, st.shape, 0)
            qry = lax.broadcasted_iota(jnp.int32, st.shape, 1)
            st = jnp.where(key // CHUNK <= qry // CHUNK, st, -jnp.inf)
        m_new = jnp.maximum(m, jnp.max(st, axis=0, keepdims=True))
        alpha = jnp.exp2(m - m_new)
        p = jnp.exp2(st - m_new)
        p_ref[:, cols] = p.astype(p_ref.dtype)
        return (m_new, alpha * l + jnp.sum(p, axis=0, keepdims=True)), alpha

    def accum(p_ref, j, alpha, cols=everything):
        pv = lax.dot_general(v_ref[keys(j), :], p_ref[:, cols], tn, preferred_element_type=F32)
        acc_ref[:, cols] = alpha * acc_ref[:, cols] + pv

    def body(jj, carry):
        stats, alpha = carry
        j = 2 * jj
        scores(sb_ref, j + 1)
        accum(pb_ref, jnp.maximum(j - 1, 0), alpha)
        stats, alpha = probs(sa_ref, pa_ref, stats)
        scores(sa_ref, j + 2)
        accum(pa_ref, j, alpha)
        stats, alpha = probs(sb_ref, pb_ref, stats)
        return stats, alpha

    nq = q_ref.shape[0] // qblk

    def query_rows(qi):
        return pl.ds(pl.multiple_of(qi * qblk, qblk), qblk)

    def start_query_block(qi):
        qt_ref[...] = q_ref[query_rows(qi), :].T
        scores(sa_ref, 0)

    def query_block(qi, _):
        acc_ref[...] = jnp.zeros_like(acc_ref)
        pb_ref[...] = jnp.zeros_like(pb_ref)
        stats = (jnp.full((1, qblk), -jnp.inf, F32), jnp.zeros((1, qblk), F32))
        nfull = qi * (qblk // kblk)
        stats, alpha = lax.fori_loop(0, nfull // 2, body, (stats, jnp.ones((1, qblk), F32)))
        late = slice(kblk, qblk)
        scores(sb_ref, nfull + 1, late)
        accum(pb_ref, jnp.maximum(nfull - 1, 0), alpha)
        (m, l), alpha = probs(sa_ref, pa_ref, stats, masked=True)
        start_query_block(jnp.minimum(qi + 1, nq - 1))
        accum(pa_ref, nfull, alpha)
        (_, l_late), alpha = probs(sb_ref, pb_ref, (m[:, late], l[:, late]), late, masked=True)
        accum(pb_ref, nfull + 1, alpha, late)
        l = jnp.concatenate([l[:, :kblk], l_late], axis=1)
        o_ref[:, query_rows(qi)] = (acc_ref[...] / l).astype(o_ref.dtype)
        return 0

    start_query_block(0)
    lax.fori_loop(0, nq, query_block, 0)


def _mla_attn(q, k, v, batch, seq):
    t = q.shape[0]
    qblk = 1024
    kblk = qblk // 2
    per_head = lambda b, h: (b, h)
    return pl.pallas_call(
        functools.partial(_mla_attn_kernel, qblk=qblk, kblk=kblk),
        grid=(batch, MLA_HEADS),
        in_specs=[pl.BlockSpec((seq, MLA_QK_PAD), per_head),
                  pl.BlockSpec((seq, MLA_QK_PAD), per_head),
                  pl.BlockSpec((seq, MLA_DV), per_head)],
        out_specs=pl.BlockSpec((MLA_DV, seq), lambda b, h: (h, b)),
        out_shape=jax.ShapeDtypeStruct((MLA_WIDTH, t), BF16),
        scratch_shapes=[pltpu.VMEM((kblk, qblk), F32), pltpu.VMEM((kblk, qblk), F32),
                        pltpu.VMEM((kblk, qblk), BF16), pltpu.VMEM((kblk, qblk), BF16),
                        pltpu.VMEM((MLA_DV, qblk), F32), pltpu.VMEM((MLA_QK_PAD, qblk), BF16)],
        compiler_params=_params("parallel", "arbitrary"),
        name="mla_attn",
    )(q, k, v)


def _layer_norm(y, g, b):
    mu = jnp.mean(y, axis=-1, keepdims=True)
    dev = y - mu
    var = jnp.mean(dev * dev, axis=-1, keepdims=True)
    return dev * lax.rsqrt(var + EPS) * g + b


def _out_ln_kernel(ret_ref, mlat_ref, x_ref, w_ref, g_ref, b_ref, o_ref, ob_ref, wb_ref, *, alpha):
    @pl.when(pl.program_id(0) == 0)
    def _():
        wb_ref[...] = w_ref[...].astype(wb_ref.dtype)

    kr = ret_ref.shape[1]
    rows = x_ref.shape[0] // OUT_LN_SPLIT
    for r in range(OUT_LN_SPLIT):
        sl = slice(r * rows, (r + 1) * rows)
        mix = jnp.dot(ret_ref[sl, :], wb_ref[0:kr, :], preferred_element_type=F32)
        mix = mix + lax.dot_general(mlat_ref[:, sl], wb_ref[kr:, :], (((0,), (0,)), ((), ())),
                                    preferred_element_type=F32)
        y = alpha * x_ref[sl, :] + mix
        out = _layer_norm(y, g_ref[...], b_ref[...])
        o_ref[sl, :] = out
        ob_ref[sl, :] = out.astype(ob_ref.dtype)


def _out_ln(ret, mlat, xf, w, g, b, alpha):
    t, d = xf.shape
    tm = 512
    row = lambda i: (i, 0)
    whole = lambda i: (0, 0)
    return pl.pallas_call(
        functools.partial(_out_ln_kernel, alpha=alpha),
        grid=(t // tm,),
        in_specs=[pl.BlockSpec((tm, ret.shape[1]), row),
                  pl.BlockSpec((mlat.shape[0], tm), lambda i: (0, i)),
                  pl.BlockSpec((tm, d), row),
                  pl.BlockSpec(w.shape, whole, pipeline_mode=pl.Buffered(1)),
                  pl.BlockSpec((1, d), whole), pl.BlockSpec((1, d), whole)],
        out_specs=[pl.BlockSpec((tm, d), row), pl.BlockSpec((tm, d), row)],
        out_shape=[jax.ShapeDtypeStruct((t, d), F32), jax.ShapeDtypeStruct((t, d), BF16)],
        scratch_shapes=[pltpu.VMEM(w.shape, BF16)],
        compiler_params=_params("arbitrary"),
        name="out_ln",
    )(ret, mlat, xf, w, g, b)


def _ffn_ln_kernel(x_ref, xb_ref, wu_ref, wd_ref, g_ref, b_ref, o_ref, acc_ref, y_ref, *, alpha, nf):
    t = pl.program_id(0)

    def mlp():
        u = jnp.dot(xb_ref[...], wu_ref[...], preferred_element_type=F32)
        a = jnp.square(jnp.maximum(u, 0.0)).astype(BF16)
        return jnp.dot(a, wd_ref[...], preferred_element_type=F32)

    @pl.when(t == 0)
    def _():
        acc_ref[...] = jnp.zeros_like(acc_ref)

    @pl.when(t % nf == 0)
    def _():
        y_ref[...] = alpha * x_ref[...] + acc_ref[...]
        o_ref[...] = _layer_norm(y_ref[...], g_ref[...], b_ref[...])
        acc_ref[...] = mlp()

    @pl.when(t % nf != 0)
    def _():
        acc_ref[...] += mlp()


def _ffn_ln(x1, x1b, wu, wd, g, b, alpha):
    t, d = x1.shape
    dff = wu.shape[1]
    tm, tf = 512, 1024
    nt, nf = t // tm, dff // tf
    last = nt * nf - 1
    rows_now = lambda s: (jnp.minimum(s, last) // nf, 0)
    rows_late = lambda s: (jnp.maximum(s - 1, 0) // nf, 0)
    hidden = lambda s: jnp.minimum(s, last) % nf
    return pl.pallas_call(
        functools.partial(_ffn_ln_kernel, alpha=alpha, nf=nf),
        grid=(nt * nf + 1,),
        in_specs=[pl.BlockSpec((tm, d), rows_late),
                  pl.BlockSpec((tm, d), rows_now),
                  pl.BlockSpec((d, tf), lambda s: (0, hidden(s))),
                  pl.BlockSpec((tf, d), lambda s: (hidden(s), 0)),
                  pl.BlockSpec((1, d), lambda s: (0, 0)),
                  pl.BlockSpec((1, d), lambda s: (0, 0))],
        out_specs=pl.BlockSpec((tm, d), rows_late),
        out_shape=jax.ShapeDtypeStruct((t, d), F32),
        scratch_shapes=[pltpu.VMEM((tm, d), F32), pltpu.VMEM((tm, d), F32)],
        compiler_params=_params("arbitrary"),
        name="ffn_ln",
    )(x1, x1b, wu, wd, g, b)


def kernel(x, positions, w_in, q_norm_g, w_uq, kv_norm_g, w_uk, w_uv, ret_gn_g, w_out, ln1_g, ln1_b,
           w_up, w_down, ln2_g, ln2_b):
    batch, seq, d = x.shape
    depth = w_in.shape[0]
    t = batch * seq
    alpha = (2.0 * depth) ** 0.25
    xf = x.reshape(t, d)
    c_ret, s_ret, c_mla, sa_mla, sb_mla = _rope_tables(positions.reshape(t, 1))
    log_g = jnp.log1p(-jnp.exp2(-5.0 - jnp.arange(RET_HEADS, dtype=F32)))
    o_cq = 2 * RET_QK + 2 * RET_WIDTH
    o_ckv = o_cq + MLA_Q_RANK
    o_kr = o_ckv + MLA_KV_RANK
    for l in range(depth):
        wt = jnp.swapaxes(w_in[l], 0, 1)
        w_c = jnp.pad(wt[o_cq:], ((0, LANES - MLA_ROPE), (0, 0))).astype(BF16)
        wuq = w_uq[l].reshape(MLA_Q_RANK, MLA_HEADS, MLA_NOPE + MLA_ROPE)
        wuq = jnp.concatenate([wuq[:, :, :MLA_NOPE].reshape(MLA_Q_RANK, -1),
                               wuq[:, :, MLA_NOPE:].reshape(MLA_Q_RANK, -1)], axis=1).astype(BF16)

        qkvg, xb = _proj(xf, wt, c_ret, s_ret)
        ret = _retention(qkvg, ret_gn_g[l].reshape(1, RET_WIDTH), log_g, batch, seq)
        q, k, v = _mla_prep(xb, w_c, q_norm_g[l].reshape(1, -1),
                            kv_norm_g[l].reshape(1, -1), wuq, w_uk[l].astype(BF16),
                            w_uv[l].astype(BF16), c_mla, sa_mla, sb_mla)
        mla = _mla_attn(q, k, v, batch, seq)
        x1, x1b = _out_ln(ret, mla, xf, w_out[l], ln1_g[l].reshape(1, d),
                     ln1_b[l].reshape(1, d), alpha)
        xf = _ffn_ln(x1, x1b, w_up[l].astype(BF16), w_down[l].astype(BF16), ln2_g[l].reshape(1, d),
                     ln2_b[l].reshape(1, d), alpha)
    return xf.reshape(batch, seq, d)
```

```python
import functools

import jax
import jax.numpy as jnp
from jax import lax
from jax.experimental import pallas as pl
from jax.experimental.pallas import tpu as pltpu

F32 = jnp.float32
BF16 = jnp.bfloat16

LANES = 128
CHUNK = 64
RET_HEADS = 8
RET_DK = 128
RET_DV = 128
RET_QK = RET_HEADS * RET_DK
RET_WIDTH = RET_HEADS * RET_DV
MLA_HEADS = 8
MLA_NOPE = 128
MLA_ROPE = 64
MLA_DV = 128
MLA_Q_RANK = 768
MLA_KV_RANK = 512
MLA_QK_PAD = 256
MLA_WIDTH = MLA_HEADS * MLA_DV
MLA_Q_SCALE = (MLA_NOPE + MLA_ROPE) ** -0.5 * 1.4426950408889634
ROPE_BASE = 10000.0
EPS = 1e-5

VMEM_LIMIT = 56 * 1024 * 1024
OUT_LN_SPLIT = 2
PROJ_SPLIT = 2


def _params(*sem):
    return pltpu.CompilerParams(dimension_semantics=sem, vmem_limit_bytes=VMEM_LIMIT)


def _rope_table_kernel(pos_ref, cst_ref, cr_ref, sr_ref, cm_ref, sa_ref, sb_ref):
    pos = pos_ref[...].astype(F32)
    ang = pos * cst_ref[0:1, :]
    c = jnp.cos(ang)
    s = jnp.sin(ang)
    half = LANES // 2
    quarter = LANES // 4
    c64 = pltpu.roll(c, half, 1)
    s64 = pltpu.roll(s, half, 1)
    cr_ref[...] = c * cst_ref[1:2, :] + c64 * cst_ref[3:4, :]
    sr_ref[...] = s * cst_ref[2:3, :] + s64 * cst_ref[3:4, :]
    cm_ref[...] = c64 * cst_ref[4:5, :] + pltpu.roll(c, half + quarter, 1) * cst_ref[6:7, :]
    sa_ref[...] = s64 * cst_ref[5:6, :]
    sb_ref[...] = pltpu.roll(s, half + quarter, 1) * cst_ref[6:7, :]


def _rope_tables(pos):
    t = pos.shape[0]
    tm = 2048
    half_r = RET_DK // 2
    half_m = MLA_ROPE // 2
    inv_r = ROPE_BASE ** (-jnp.arange(0, RET_DK, 2, dtype=F32) / RET_DK)
    inv_m = ROPE_BASE ** (-jnp.arange(0, MLA_ROPE, 2, dtype=F32) / MLA_ROPE)
    assert half_r == LANES // 2 and half_m == LANES // 4
    lane = jnp.arange(LANES)
    lo64 = (lane < half_r).astype(F32)
    q0 = (lane < half_m).astype(F32)
    q1 = ((lane >= half_m) & (lane < 2 * half_m)).astype(F32)
    cst = jnp.stack([
        jnp.concatenate([inv_r, inv_m, jnp.zeros((LANES - half_r - half_m,), F32)]),
        lo64, -lo64, 1.0 - lo64, q0, -q0, q1, jnp.zeros((LANES,), F32)])
    out = jax.ShapeDtypeStruct((t, LANES), F32)
    tab_spec = pl.BlockSpec((tm, LANES), lambda i: (i, 0))
    return pl.pallas_call(
        _rope_table_kernel,
        grid=(t // tm,),
        in_specs=[pl.BlockSpec((tm, 1), lambda i: (i, 0)),
                  pl.BlockSpec((8, LANES), lambda i: (0, 0))],
        out_specs=[tab_spec] * 5,
        out_shape=[out] * 5,
        compiler_params=_params("parallel"),
        name="rope_tables",
    )(pos, cst)


def _proj_kernel(x_ref, wt_ref, c_ref, s_ref, o_ref, xb_ref, wb_ref, *, k_scale):
    j = pl.program_id(0)

    @pl.when(pl.program_id(1) == 0)
    def _():
        wb_ref[...] = wt_ref[...].astype(wb_ref.dtype)

    rotary = j < 2
    scale = jnp.where(j == 1, k_scale, 1.0).astype(F32)
    rows = x_ref.shape[0] // PROJ_SPLIT
    nt_dims = (((1,), (1,)), ((), ()))
    for r0 in range(0, x_ref.shape[0], rows):
        sl = slice(r0, r0 + rows)
        xb = x_ref[sl, :].astype(xb_ref.dtype)
        res = lax.dot_general(xb, wb_ref[...], nt_dims, preferred_element_type=F32)
        c = jnp.where(rotary, c_ref[sl, :] * scale, 1.0)
        s = jnp.where(rotary, s_ref[sl, :] * scale, 0.0)
        for h in range(res.shape[1] // RET_DK):
            r = res[:, h * RET_DK:(h + 1) * RET_DK]
            o = r * c + pltpu.roll(r, RET_DK // 2, 1) * s
            o_ref[sl, h * RET_DK:(h + 1) * RET_DK] = o.astype(o_ref.dtype)

    @pl.when(j == 0)
    def _():
        xb_ref[...] = x_ref[...].astype(xb_ref.dtype)


def _proj(xf, wt, c_tab, s_tab):
    t, d = xf.shape
    tm, tn = 1024, RET_QK
    n = 4 * tn
    nt = t // tm
    assert RET_QK == RET_WIDTH and wt.shape[0] >= n
    rows = lambda j, i: (i, 0)
    xb_rows = lambda j, i: (jnp.where(j == 0, i, nt - 1), 0)
    return pl.pallas_call(
        functools.partial(_proj_kernel, k_scale=RET_DK ** -0.5),
        grid=(n // tn, nt),
        in_specs=[pl.BlockSpec((tm, d), rows),
                  pl.BlockSpec((tn, d), lambda j, i: (j, 0)),
                  pl.BlockSpec((tm, LANES), rows),
                  pl.BlockSpec((tm, LANES), rows)],
        out_specs=[pl.BlockSpec((tm, tn), lambda j, i: (i, j)),
                   pl.BlockSpec((tm, d), xb_rows)],
        out_shape=[jax.ShapeDtypeStruct((t, n), BF16), jax.ShapeDtypeStruct((t, d), BF16)],
        scratch_shapes=[pltpu.VMEM((tn, d), BF16)],
        compiler_params=_params("arbitrary", "arbitrary"),
        name="proj",
    )(xf, wt, c_tab, s_tab)


def _retention_kernel(lg_ref, q_ref, k_ref, v_ref, g_ref, gn_ref, o_ref, d_ref, *, blk):
    lg = lg_ref[pl.program_id(1)]
    row = lax.broadcasted_iota(jnp.int32, (blk, blk), 0)
    col = lax.broadcasted_iota(jnp.int32, (blk, blk), 1)
    dist = jnp.abs(row - col).astype(F32)
    d_ref[...] = jnp.where(col // CHUNK <= row // CHUNK, jnp.exp(lg * dist), 0.0)
    idx = lax.broadcasted_iota(jnp.int32, (blk, 1), 0).astype(F32)
    q_dec = jnp.exp(lg * (idx + 1.0))
    k_dec = jnp.exp(lg * (blk - 1.0 - idx))
    blk_dec = jnp.exp(jnp.full((1, 1), blk, F32) * lg)
    gn = gn_ref[...]
    nt = (((1,), (1,)), ((), ()))
    tn = (((0,), (0,)), ((), ()))

    def body(n, state):
        sl = pl.ds(pl.multiple_of(n * blk, blk), blk)
        q = q_ref[sl, :]
        k = k_ref[sl, :]
        v = v_ref[sl, :]
        s = lax.dot_general(q, k, nt, preferred_element_type=F32) * d_ref[...]
        intra = jnp.dot(s.astype(BF16), v, preferred_element_type=F32)
        cross = jnp.dot(q, state.astype(BF16), preferred_element_type=F32) * q_dec
        kd = (k.astype(F32) * k_dec).astype(BF16)
        new_state = blk_dec * state + lax.dot_general(kd, v, tn, preferred_element_type=F32)
        ret = intra + cross
        mu = jnp.mean(ret, axis=-1, keepdims=True)
        dev = ret - mu
        var = jnp.mean(dev * dev, axis=-1, keepdims=True)
        y = dev * lax.rsqrt(var + EPS) * gn
        g = g_ref[sl, :].astype(F32)
        gate = g / (1.0 + jnp.exp(-g))
        o_ref[sl, :] = (gate * y).astype(o_ref.dtype)
        return new_state

    lax.fori_loop(0, q_ref.shape[0] // blk, body, jnp.zeros((RET_DK, RET_DV), F32), unroll=True)


def _retention(qkvg, gn_g, log_g, batch, seq):
    t = qkvg.shape[0]
    blk = 256
    h = RET_HEADS
    return pl.pallas_call(
        functools.partial(_retention_kernel, blk=blk),
        grid=(batch, h),
        in_specs=[pl.BlockSpec(memory_space=pltpu.SMEM),
                  pl.BlockSpec((seq, RET_DK), lambda b, i: (b, i)),
                  pl.BlockSpec((seq, RET_DK), lambda b, i: (b, h + i)),
                  pl.BlockSpec((seq, RET_DV), lambda b, i: (b, 2 * h + i)),
                  pl.BlockSpec((seq, RET_DV), lambda b, i: (b, 3 * h + i)),
                  pl.BlockSpec((1, RET_DV), lambda b, i: (0, i))],
        out_specs=pl.BlockSpec((seq, RET_DV), lambda b, i: (b, i)),
        out_shape=jax.ShapeDtypeStruct((t, RET_WIDTH), BF16),
        scratch_shapes=[pltpu.VMEM((blk, blk), F32)],
        compiler_params=_params("parallel", "arbitrary"),
        name="retention",
    )(log_g, qkvg, qkvg, qkvg, qkvg, gn_g)


def _rms(v, g):
    return v * lax.rsqrt(jnp.mean(v * v, axis=-1, keepdims=True) + EPS) * g


def _rope64(r, cm, sa, sb):
    half = MLA_ROPE // 2
    return r * cm + pltpu.roll(r, LANES - half, 1) * sa + pltpu.roll(r, half, 1) * sb


def _mla_prep_kernel(x_ref, wc_ref, qg_ref, kvg_ref, wuq_ref, wuk_ref, wuv_ref,
                     cm_ref, sa_ref, sb_ref, q_ref, k_ref, v_ref):
    r_ckv = MLA_Q_RANK
    r_kr = r_ckv + MLA_KV_RANK
    wcq_ref = wc_ref.at[0:r_ckv, :]
    wckv_ref = wc_ref.at[r_ckv:r_kr, :]
    wkr_ref = wc_ref.at[r_kr:r_kr + LANES, :]
    x = x_ref[...]
    cm = cm_ref[...]
    sa = sa_ref[...]
    sb = sb_ref[...]
    nt = (((1,), (1,)), ((), ()))
    cq = lax.dot_general(x, wcq_ref[...], nt, preferred_element_type=F32)
    cqn = _rms(cq, qg_ref[...]).astype(BF16)
    q = jnp.dot(cqn, wuq_ref[...], preferred_element_type=F32) * MLA_Q_SCALE
    rope0 = MLA_HEADS * MLA_NOPE
    for h in range(MLA_HEADS):
        lo = h * MLA_QK_PAD
        q_ref[:, lo:lo + MLA_NOPE] = q[:, h * MLA_NOPE:(h + 1) * MLA_NOPE].astype(q_ref.dtype)
        pair = q[:, rope0 + (h // 2) * LANES:rope0 + (h // 2 + 1) * LANES]
        if h % 2:
            pair = pltpu.roll(pair, LANES // 2, 1)
        q_ref[:, lo + MLA_NOPE:lo + MLA_QK_PAD] = _rope64(pair, cm, sa, sb).astype(q_ref.dtype)
    ckv = lax.dot_general(x, wckv_ref[...], nt, preferred_element_type=F32)
    ckvn = _rms(ckv, kvg_ref[...]).astype(BF16)
    kr = lax.dot_general(x, wkr_ref[...], nt, preferred_element_type=F32)
    k_rope = _rope64(kr, cm, sa, sb).astype(k_ref.dtype)
    kn = jnp.dot(ckvn, wuk_ref[...], preferred_element_type=F32)
    for h in range(MLA_HEADS):
        lo = h * MLA_QK_PAD
        k_ref[:, lo:lo + MLA_NOPE] = kn[:, h * MLA_NOPE:(h + 1) * MLA_NOPE].astype(k_ref.dtype)
        k_ref[:, lo + MLA_NOPE:lo + MLA_QK_PAD] = k_rope
    v_ref[...] = jnp.dot(ckvn, wuv_ref[...], preferred_element_type=F32).astype(v_ref.dtype)


def _mla_prep(xb, wc, qg, kvg, wuq, wuk, wuv, cm, sa, sb):
    t, d = xb.shape
    tm = 1024
    row = lambda i: (i, 0)
    whole = lambda i: (0, 0)
    full = lambda a: pl.BlockSpec(a.shape, whole, pipeline_mode=pl.Buffered(1))
    qk_w = MLA_HEADS * MLA_QK_PAD
    return pl.pallas_call(
        _mla_prep_kernel,
        grid=(t // tm,),
        in_specs=[pl.BlockSpec((tm, d), row), full(wc), full(qg), full(kvg),
                  full(wuq), full(wuk), full(wuv),
                  pl.BlockSpec((tm, LANES), row), pl.BlockSpec((tm, LANES), row),
                  pl.BlockSpec((tm, LANES), row)],
        out_specs=[pl.BlockSpec((tm, qk_w), row), pl.BlockSpec((tm, qk_w), row),
                   pl.BlockSpec((tm, MLA_WIDTH), row)],
        out_shape=[jax.ShapeDtypeStruct((t, qk_w), BF16), jax.ShapeDtypeStruct((t, qk_w), BF16),
                   jax.ShapeDtypeStruct((t, MLA_WIDTH), BF16)],
        compiler_params=_params("parallel"),
        name="mla_prep",
    )(xb, wc, qg, kvg, wuq, wuk, wuv, cm, sa, sb)


def _mla_attn_kernel(q_ref, k_ref, v_ref, wu_ref, wd_ref, o_ref, wub_ref, wdb_ref,
                     sa_ref, sb_ref, pa_ref, pb_ref, acc_ref, qt_ref, *, qblk, kblk):
    wub_ref[...] = wu_ref[...].astype(wub_ref.dtype)
    wdb_ref[...] = wd_ref[...].astype(wdb_ref.dtype)
    tn = (((0,), (0,)), ((), ()))

    def keys(j):
        return pl.ds(pl.multiple_of(j * kblk, kblk), kblk)

    everything = slice(None)

    def scores(s_ref, j, cols=everything):
        s_ref[:, cols] = jnp.dot(k_ref[keys(j), :], qt_ref[:, cols], preferred_element_type=F32)

    def probs(s_ref, p_ref, stats, cols=everything, masked=False):
        m, l = stats
        st = s_ref[:, cols]
        if masked:
            key = lax.broadcasted_iota(jnp.int32, st.shape, 0)
            qry = lax.broadcasted_iota(jnp.int32, st.shape, 1)
            st = jnp.where(key // CHUNK <= qry // CHUNK, st, -jnp.inf)
        m_new = jnp.maximum(m, jnp.max(st, axis=0, keepdims=True))
        alpha = jnp.exp2(m - m_new)
        p = jnp.exp2(st - m_new)
        p_ref[:, cols] = p.astype(p_ref.dtype)
        return (m_new, alpha * l + jnp.sum(p, axis=0, keepdims=True)), alpha

    def accum(p_ref, j, alpha, cols=everything):
        pv = lax.dot_general(v_ref[keys(j), :], p_ref[:, cols], tn, preferred_element_type=F32)
        acc_ref[:, cols] = alpha * acc_ref[:, cols] + pv

    def body(jj, carry):
        stats, alpha = carry
        j = 2 * jj
        scores(sb_ref, j + 1)
        accum(pb_ref, jnp.maximum(j - 1, 0), alpha)
        stats, alpha = probs(sa_ref, pa_ref, stats)
        scores(sa_ref, j + 2)
        accum(pa_ref, j, alpha)
        stats, alpha = probs(sb_ref, pb_ref, stats)
        return stats, alpha

    nq = q_ref.shape[0] // qblk

    def query_rows(qi):
        return pl.ds(pl.multiple_of(qi * qblk, qblk), qblk)

    def start_query_block(qi):
        qt_ref[...] = q_ref[query_rows(qi), :].T
        scores(sa_ref, 0)

    def query_block(qi, _):
        acc_ref[...] = jnp.zeros_like(acc_ref)
        pb_ref[...] = jnp.zeros_like(pb_ref)
        stats = (jnp.full((1, qblk), -jnp.inf, F32), jnp.zeros((1, qblk), F32))
        nfull = qi * (qblk // kblk)
        stats, alpha = lax.fori_loop(0, nfull // 2, body, (stats, jnp.ones((1, qblk), F32)))
        late = slice(kblk, qblk)
        scores(sb_ref, nfull + 1, late)
        accum(pb_ref, jnp.maximum(nfull - 1, 0), alpha)
        (m, l), alpha = probs(sa_ref, pa_ref, stats, masked=True)
        start_query_block(jnp.minimum(qi + 1, nq - 1))
        accum(pa_ref, nfull, alpha)
        (_, l_late), alpha = probs(sb_ref, pb_ref, (m[:, late], l[:, late]), late, masked=True)
        accum(pb_ref, nfull + 1, alpha, late)
        l = jnp.concatenate([l[:, :kblk], l_late], axis=1)
        o_ref[:, query_rows(qi)] = (acc_ref[...] / l).astype(o_ref.dtype)
        return 0

    start_query_block(0)
    lax.fori_loop(0, nq, query_block, 0)


def _mla_attn(q, k, v, w_up, w_down, batch, seq):
    t = q.shape[0]
    qblk = 1024
    kblk = qblk // 2
    per_head = lambda b, h: (b, h)
    steps = batch * MLA_HEADS
    slab = lambda b, h: (b * MLA_HEADS + h, 0)
    up_rows, down_rows = w_up.shape[0] // steps, w_down.shape[0] // steps
    up_spec = pl.BlockSpec((up_rows, w_up.shape[1]), slab)
    down_spec = pl.BlockSpec((down_rows, w_down.shape[1]), slab)
    return pl.pallas_call(
        functools.partial(_mla_attn_kernel, qblk=qblk, kblk=kblk),
        grid=(batch, MLA_HEADS),
        in_specs=[pl.BlockSpec((seq, MLA_QK_PAD), per_head),
                  pl.BlockSpec((seq, MLA_QK_PAD), per_head),
                  pl.BlockSpec((seq, MLA_DV), per_head), up_spec, down_spec],
        out_specs=[pl.BlockSpec((MLA_DV, seq), lambda b, h: (h, b)), up_spec, down_spec],
        out_shape=[jax.ShapeDtypeStruct((MLA_WIDTH, t), BF16),
                   jax.ShapeDtypeStruct(w_up.shape, BF16), jax.ShapeDtypeStruct(w_down.shape, BF16)],
        scratch_shapes=[pltpu.VMEM((kblk, qblk), F32), pltpu.VMEM((kblk, qblk), F32),
                        pltpu.VMEM((kblk, qblk), BF16), pltpu.VMEM((kblk, qblk), BF16),
                        pltpu.VMEM((MLA_DV, qblk), F32), pltpu.VMEM((MLA_QK_PAD, qblk), BF16)],
        compiler_params=_params("parallel", "arbitrary"),
        name="mla_attn",
    )(q, k, v, w_up, w_down)


def _layer_norm(y, g, b):
    mu = jnp.mean(y, axis=-1, keepdims=True)
    dev = y - mu
    var = jnp.mean(dev * dev, axis=-1, keepdims=True)
    return dev * lax.rsqrt(var + EPS) * g + b


def _out_ln_kernel(ret_ref, mlat_ref, x_ref, w_ref, g_ref, b_ref, o_ref, ob_ref, wb_ref, *, alpha):
    @pl.when(pl.program_id(0) == 0)
    def _():
        wb_ref[...] = w_ref[...].astype(wb_ref.dtype)

    kr = ret_ref.shape[1]
    rows = x_ref.shape[0] // OUT_LN_SPLIT
    for r in range(OUT_LN_SPLIT):
        sl = slice(r * rows, (r + 1) * rows)
        mix = jnp.dot(ret_ref[sl, :], wb_ref[0:kr, :], preferred_element_type=F32)
        mix = mix + lax.dot_general(mlat_ref[:, sl], wb_ref[kr:, :], (((0,), (0,)), ((), ())),
                                    preferred_element_type=F32)
        y = alpha * x_ref[sl, :] + mix
        out = _layer_norm(y, g_ref[...], b_ref[...])
        o_ref[sl, :] = out
        ob_ref[sl, :] = out.astype(ob_ref.dtype)


def _out_ln(ret, mlat, xf, w, g, b, alpha):
    t, d = xf.shape
    tm = 512
    row = lambda i: (i, 0)
    whole = lambda i: (0, 0)
    return pl.pallas_call(
        functools.partial(_out_ln_kernel, alpha=alpha),
        grid=(t // tm,),
        in_specs=[pl.BlockSpec((tm, ret.shape[1]), row),
                  pl.BlockSpec((mlat.shape[0], tm), lambda i: (0, i)),
                  pl.BlockSpec((tm, d), row),
                  pl.BlockSpec(w.shape, whole, pipeline_mode=pl.Buffered(1)),
                  pl.BlockSpec((1, d), whole), pl.BlockSpec((1, d), whole)],
        out_specs=[pl.BlockSpec((tm, d), row), pl.BlockSpec((tm, d), row)],
        out_shape=[jax.ShapeDtypeStruct((t, d), F32), jax.ShapeDtypeStruct((t, d), BF16)],
        scratch_shapes=[pltpu.VMEM(w.shape, BF16)],
        compiler_params=_params("arbitrary"),
        name="out_ln",
    )(ret, mlat, xf, w, g, b)


def _ffn_ln_kernel(x_ref, xb_ref, wu_ref, wd_ref, g_ref, b_ref, o_ref, acc_ref, y_ref, *, alpha, nf):
    t = pl.program_id(0)

    def mlp():
        u = jnp.dot(xb_ref[...], wu_ref[...], preferred_element_type=F32)
        a = jnp.square(jnp.maximum(u, 0.0)).astype(BF16)
        return jnp.dot(a, wd_ref[...], preferred_element_type=F32)

    @pl.when(t == 0)
    def _():
        acc_ref[...] = jnp.zeros_like(acc_ref)

    @pl.when(t % nf == 0)
    def _():
        y_ref[...] = alpha * x_ref[...] + acc_ref[...]
        o_ref[...] = _layer_norm(y_ref[...], g_ref[...], b_ref[...])
        acc_ref[...] = mlp()

    @pl.when(t % nf != 0)
    def _():
        acc_ref[...] += mlp()


def _ffn_ln(x1, x1b, wu, wd, g, b, alpha):
    t, d = x1.shape
    dff = wu.shape[1]
    tm, tf = 512, 1024
    nt, nf = t // tm, dff // tf
    last = nt * nf - 1
    rows_now = lambda s: (jnp.minimum(s, last) // nf, 0)
    rows_late = lambda s: (jnp.maximum(s - 1, 0) // nf, 0)
    hidden = lambda s: jnp.minimum(s, last) % nf
    return pl.pallas_call(
        functools.partial(_ffn_ln_kernel, alpha=alpha, nf=nf),
        grid=(nt * nf + 1,),
        in_specs=[pl.BlockSpec((tm, d), rows_late),
                  pl.BlockSpec((tm, d), rows_now),
                  pl.BlockSpec((d, tf), lambda s: (0, hidden(s))),
                  pl.BlockSpec((tf, d), lambda s: (hidden(s), 0)),
                  pl.BlockSpec((1, d), lambda s: (0, 0)),
                  pl.BlockSpec((1, d), lambda s: (0, 0))],
        out_specs=pl.BlockSpec((tm, d), rows_late),
        out_shape=jax.ShapeDtypeStruct((t, d), F32),
        scratch_shapes=[pltpu.VMEM((tm, d), F32), pltpu.VMEM((tm, d), F32)],
        compiler_params=_params("arbitrary"),
        name="ffn_ln",
    )(x1, x1b, wu, wd, g, b)


def kernel(x, positions, w_in, q_norm_g, w_uq, kv_norm_g, w_uk, w_uv, ret_gn_g, w_out, ln1_g, ln1_b,
           w_up, w_down, ln2_g, ln2_b):
    batch, seq, d = x.shape
    depth = w_in.shape[0]
    t = batch * seq
    alpha = (2.0 * depth) ** 0.25
    xf = x.reshape(t, d)
    c_ret, s_ret, c_mla, sa_mla, sb_mla = _rope_tables(positions.reshape(t, 1))
    log_g = jnp.log1p(-jnp.exp2(-5.0 - jnp.arange(RET_HEADS, dtype=F32)))
    o_cq = 2 * RET_QK + 2 * RET_WIDTH
    o_ckv = o_cq + MLA_Q_RANK
    o_kr = o_ckv + MLA_KV_RANK
    for l in range(depth):
        wt = jnp.swapaxes(w_in[l], 0, 1)
        w_c = jnp.pad(wt[o_cq:], ((0, LANES - MLA_ROPE), (0, 0))).astype(BF16)
        wuq = w_uq[l].reshape(MLA_Q_RANK, MLA_HEADS, MLA_NOPE + MLA_ROPE)
        wuq = jnp.concatenate([wuq[:, :, :MLA_NOPE].reshape(MLA_Q_RANK, -1),
                               wuq[:, :, MLA_NOPE:].reshape(MLA_Q_RANK, -1)], axis=1).astype(BF16)

        qkvg, xb = _proj(xf, wt, c_ret, s_ret)
        ret = _retention(qkvg, ret_gn_g[l].reshape(1, RET_WIDTH), log_g, batch, seq)
        q, k, v = _mla_prep(xb, w_c, q_norm_g[l].reshape(1, -1),
                            kv_norm_g[l].reshape(1, -1), wuq, w_uk[l].astype(BF16),
                            w_uv[l].astype(BF16), c_mla, sa_mla, sb_mla)
        mla, w_up_b, w_down_b = _mla_attn(q, k, v, w_up[l], w_down[l], batch, seq)
        x1, x1b = _out_ln(ret, mla, xf, w_out[l], ln1_g[l].reshape(1, d),
                     ln1_b[l].reshape(1, d), alpha)
        xf = _ffn_ln(x1, x1b, w_up_b, w_down_b, ln2_g[l].reshape(1, d),
                     ln2_b[l].reshape(1, d), alpha)
    return xf.reshape(batch, seq, d)
```
